```python
import math
import jax, jax.numpy as jnp
from jax import lax
import numpy as np

D_MODEL = 1024
BATCH = 8
SEQ = 8192
DEPTH = 1

GRID_W = 64
WIN_H = 8
WIN_W = 16
HEAD_DIM = 64
ATTN_WIDTH = D_MODEL // 2
ATTN_HEADS = ATTN_WIDTH // HEAD_DIM
S5_WIDTH = D_MODEL - ATTN_WIDTH
S5_GROUP_CH = 16
S5_GROUPS = S5_WIDTH // S5_GROUP_CH
S5_STATE = 64
DT_MIN = 1e-3
DT_MAX = 1e-1
MIX_WIDTH = ATTN_WIDTH + S5_WIDTH
IN_WIDTH = 3 * ATTN_WIDTH + S5_WIDTH
N_EXPERT_GROUPS = 8
EXPERTS_PER_GROUP = 8
N_EXPERTS = N_EXPERT_GROUPS * EXPERTS_PER_GROUP
TOP_K_IN_GROUP = 2
D_EXPERT = D_MODEL // 2
MOE_BLOCK = 128
LN_EPS = 1e-5

kernel_name = "hybrid_natten_s5_hiermoe_deepnorm"


def layer_norm(x, g, b):
    x32 = x.astype(jnp.float32)
    mu = jnp.mean(x32, axis=-1, keepdims=True)
    var = jnp.mean(jnp.square(x32 - mu), axis=-1, keepdims=True)
    y = (x32 - mu) * lax.rsqrt(var + LN_EPS) * g.astype(jnp.float32) + b.astype(jnp.float32)
    return y.astype(x.dtype)


def neighbourhood_attention(q, k, v, rpb):
    bsz, seq, heads, dh = q.shape
    rows = seq // GRID_W
    kh = min(WIN_H, rows)
    qg = q.reshape(bsz, rows, GRID_W, heads, dh) * (dh ** -0.5)
    kg = k.reshape(bsz, rows, GRID_W, heads, dh)
    vg = v.reshape(bsz, rows, GRID_W, heads, dh)
    col = np.arange(GRID_W)
    cstart = np.clip(col - WIN_W // 2, 0, GRID_W - WIN_W)
    cidx = cstart[:, None] + np.arange(WIN_W)[None, :]
    dcol = cidx - col[:, None] + (WIN_W - 1)
    rpb_c = rpb.astype(jnp.float32)[:, :, dcol]

    def one_row(r):
        rs = jnp.clip(r - kh // 2, 0, rows - kh)
        q_r = lax.dynamic_index_in_dim(qg, r, axis=1, keepdims=False)
        k_band = lax.dynamic_slice_in_dim(kg, rs, kh, axis=1)
        v_band = lax.dynamic_slice_in_dim(vg, rs, kh, axis=1)
        k_win = k_band[:, :, cidx]
        v_win = v_band[:, :, cidx]
        s = jnp.einsum('bchd,bicjhd->bhcij', q_r, k_win).astype(jnp.float32)
        drow = rs + jnp.arange(kh) - r + (WIN_H - 1)
        bias = jnp.take(rpb_c, drow, axis=1)
        s = s + jnp.transpose(bias, (0, 2, 1, 3))[None]
        p = jax.nn.softmax(s.reshape(bsz, heads, GRID_W, kh * WIN_W), axis=-1)
        p = p.reshape(bsz, heads, GRID_W, kh, WIN_W).astype(v.dtype)
        return jnp.einsum('bhcij,bicjhd->bchd', p, v_win)

    out = lax.map(one_row, jnp.arange(rows))
    return jnp.transpose(out, (1, 0, 2, 3, 4)).reshape(bsz, seq, heads * dh)


def _ssm_combine(e1, e2):
    a1r, a1i, b1r, b1i = e1
    a2r, a2i, b2r, b2i = e2
    ar = a2r * a1r - a2i * a1i
    ai = a2r * a1i + a2i * a1r
    br = a2r * b1r - a2i * b1i + b2r
    bi = a2r * b1i + a2i * b1r + b2i
    return (ar, ai, br, bi)


def s5_bidirectional(u, a_re, a_im, log_dt, b_re, b_im, c_re, c_im, d_skip):
    f32 = jnp.float32
    u32 = u.astype(f32)
    seq = u.shape[1]
    y = d_skip.astype(f32)[None, None] * u32
    for direction in range(2):
        ar = a_re[direction].astype(f32)
        ai = a_im[direction].astype(f32)
        dt = jnp.exp(log_dt[direction].astype(f32))[:, None]
        mag = jnp.exp(ar * dt)
        lr = mag * jnp.cos(ai * dt)
        li = mag * jnp.sin(ai * dt)
        den = ar * ar + ai * ai
        zr = ((lr - 1.0) * ar + li * ai) / den
        zi = (li * ar - (lr - 1.0) * ai) / den
        br = b_re[direction].astype(f32)
        bi = b_im[direction].astype(f32)
        bbr = zr[..., None] * br - zi[..., None] * bi
        bbi = zr[..., None] * bi + zi[..., None] * br
        xr = jnp.einsum('bsgc,gpc->bsgp', u32, bbr)
        xi = jnp.einsum('bsgc,gpc->bsgp', u32, bbi)
        shp = (1, seq) + lr.shape
        lam_r = jnp.broadcast_to(lr[None, None], shp)
        lam_i = jnp.broadcast_to(li[None, None], shp)
        _, _, hr, hi = lax.associative_scan(
            _ssm_combine, (lam_r, lam_i, xr, xi), reverse=(direction == 1), axis=1)
        y = y + jnp.einsum('bsgp,gcp->bsgc', hr, c_re[direction].astype(f32)) \
              - jnp.einsum('bsgp,gcp->bsgc', hi, c_im[direction].astype(f32))
    return y


def hybrid_mixer(h, w_in, rpb, s5_a_re, s5_a_im, s5_log_dt, s5_b_re, s5_b_im,
                 s5_c_re, s5_c_im, s5_d, w_glu, b_glu, w_out):
    bsz, seq, _ = h.shape
    proj = h @ w_in
    q = proj[..., :ATTN_WIDTH].reshape(bsz, seq, ATTN_HEADS, HEAD_DIM)
    k = proj[..., ATTN_WIDTH:2 * ATTN_WIDTH].reshape(bsz, seq, ATTN_HEADS, HEAD_DIM)
    v = proj[..., 2 * ATTN_WIDTH:3 * ATTN_WIDTH].reshape(bsz, seq, ATTN_HEADS, HEAD_DIM)
    u = proj[..., 3 * ATTN_WIDTH:].reshape(bsz, seq, S5_GROUPS, S5_GROUP_CH)
    attn = neighbourhood_attention(q, k, v, rpb)
    ssm = s5_bidirectional(u, s5_a_re, s5_a_im, s5_log_dt, s5_b_re, s5_b_im,
                           s5_c_re, s5_c_im, s5_d).reshape(bsz, seq, S5_WIDTH)
    ssm = jax.nn.gelu(ssm).astype(h.dtype)
    ssm = ssm * jax.nn.sigmoid(ssm @ w_glu + b_glu)
    return jnp.concatenate([attn, ssm], axis=-1) @ w_out


def hierarchical_moe(h, w_rg, b_rg, w_re, b_re, w_gate, w_up, w_down):
    bsz, seq, dm = h.shape
    n_tok = bsz * seq
    t = h.reshape(n_tok, dm)
    g_prob = jax.nn.softmax((t @ w_rg).astype(jnp.float32) + b_rg.astype(jnp.float32), axis=-1)
    g_val, g_idx = lax.top_k(g_prob, 1)
    e_logits = ((t @ w_re).astype(jnp.float32) + b_re.astype(jnp.float32))
    e_logits = e_logits.reshape(n_tok, N_EXPERT_GROUPS, EXPERTS_PER_GROUP)
    e_in = jnp.take_along_axis(e_logits, g_idx[:, :, None], axis=1)[:, 0]
    e_top, e_loc = lax.top_k(e_in, TOP_K_IN_GROUP)
    gate = g_val * jax.nn.softmax(e_top, axis=-1)
    expert = g_idx * EXPERTS_PER_GROUP + e_loc

    n_assign = n_tok * TOP_K_IN_GROUP
    flat_e = expert.reshape(-1).astype(jnp.int32)
    flat_tok = jnp.repeat(jnp.arange(n_tok, dtype=jnp.int32), TOP_K_IN_GROUP)
    flat_w = gate.reshape(-1)
    order = jnp.argsort(flat_e)
    se, stok, sw = flat_e[order], flat_tok[order], flat_w[order]
    counts = jnp.bincount(flat_e, length=N_EXPERTS)
    starts = jnp.cumsum(counts) - counts
    padded = (counts + MOE_BLOCK - 1) // MOE_BLOCK * MOE_BLOCK
    pends = jnp.cumsum(padded)
    pstarts = pends - padded
    dest = pstarts[se] + (jnp.arange(n_assign, dtype=jnp.int32) - starts[se])
    n_blocks = (n_assign + N_EXPERTS * (MOE_BLOCK - 1) + MOE_BLOCK - 1) // MOE_BLOCK
    buf = jnp.zeros((n_blocks * MOE_BLOCK, dm), t.dtype).at[dest].set(t[stok])
    block_expert = jnp.minimum(
        jnp.searchsorted(pends, jnp.arange(n_blocks) * MOE_BLOCK, side='right'), N_EXPERTS - 1)

    def expert_block(args):
        xb, e = args
        hid = jax.nn.silu(xb @ w_gate[e]) * (xb @ w_up[e])
        return hid @ w_down[e]

    out = lax.map(expert_block, (buf.reshape(n_blocks, MOE_BLOCK, dm), block_expert))
    out = out.reshape(n_blocks * MOE_BLOCK, dm)
    contrib = out[dest].astype(jnp.float32) * sw[:, None]
    y = jnp.zeros((n_tok, dm), jnp.float32).at[stok].add(contrib)
    return y.astype(h.dtype).reshape(bsz, seq, dm)


def setup_inputs(seed: int = 0) -> dict:
    key = jax.random.key(seed)
    ks = jax.random.split(key, 24)
    beta = (8.0 * DEPTH) ** -0.25
    L, G, P, C = DEPTH, S5_GROUPS, S5_STATE, S5_GROUP_CH
    nrm = lambda k, shape: jax.random.normal(k, shape, jnp.float32)
    x = nrm(ks[0], (BATCH, SEQ, D_MODEL))
    w_in = nrm(ks[1], (L, D_MODEL, IN_WIDTH)) * D_MODEL ** -0.5
    w_in = w_in.at[:, :, 2 * ATTN_WIDTH:3 * ATTN_WIDTH].multiply(beta)
    rpb = 0.02 * nrm(ks[2], (L, ATTN_HEADS, 2 * WIN_H - 1, 2 * WIN_W - 1))
    s5_a_re = -0.5 + 0.01 * nrm(ks[3], (L, 2, G, P))
    s5_a_im = math.pi * jnp.arange(P, dtype=jnp.float32) + 0.01 * nrm(ks[4], (L, 2, G, P))
    s5_log_dt = math.log(DT_MIN) + jax.random.uniform(ks[5], (L, 2, G), jnp.float32) * (
        math.log(DT_MAX) - math.log(DT_MIN))
    s5_b_re = nrm(ks[6], (L, 2, G, P, C)) * (2.0 * C) ** -0.5
    s5_b_im = nrm(ks[7], (L, 2, G, P, C)) * (2.0 * C) ** -0.5
    s5_c_re = nrm(ks[8], (L, 2, G, C, P)) * P ** -0.5
    s5_c_im = nrm(ks[9], (L, 2, G, C, P)) * P ** -0.5
    s5_d = nrm(ks[10], (L, G, C))
    w_glu = nrm(ks[11], (L, S5_WIDTH, S5_WIDTH)) * S5_WIDTH ** -0.5
    b_glu = 0.01 * nrm(ks[12], (L, S5_WIDTH))
    w_out = nrm(ks[13], (L, MIX_WIDTH, D_MODEL)) * MIX_WIDTH ** -0.5 * beta
    ln1_g = 1.0 + 0.01 * nrm(ks[14], (L, D_MODEL))
    ln1_b = 0.01 * nrm(ks[15], (L, D_MODEL))
    w_router_group = nrm(ks[16], (L, D_MODEL, N_EXPERT_GROUPS)) * D_MODEL ** -0.5
    b_router_group = 0.01 * nrm(ks[17], (L, N_EXPERT_GROUPS))
    w_router_expert = nrm(ks[18], (L, D_MODEL, N_EXPERTS)) * D_MODEL ** -0.5
    b_router_expert = 0.01 * nrm(ks[19], (L, N_EXPERTS))
    w_gate = nrm(ks[20], (L, N_EXPERTS, D_MODEL, D_EXPERT)) * D_MODEL ** -0.5
    w_up = nrm(ks[21], (L, N_EXPERTS, D_MODEL, D_EXPERT)) * D_MODEL ** -0.5
    w_down = nrm(ks[22], (L, N_EXPERTS, D_EXPERT, D_MODEL)) * D_EXPERT ** -0.5 * beta
    ks2 = jax.random.split(ks[23], 2)
    ln2_g = 1.0 + 0.01 * nrm(ks2[0], (L, D_MODEL))
    ln2_b = 0.01 * nrm(ks2[1], (L, D_MODEL))
    return {"x": x, "w_in": w_in, "rpb": rpb, "s5_a_re": s5_a_re, "s5_a_im": s5_a_im,
            "s5_log_dt": s5_log_dt, "s5_b_re": s5_b_re, "s5_b_im": s5_b_im,
            "s5_c_re": s5_c_re, "s5_c_im": s5_c_im, "s5_d": s5_d, "w_glu": w_glu,
            "b_glu": b_glu, "w_out": w_out, "ln1_g": ln1_g, "ln1_b": ln1_b,
            "w_router_group": w_router_group, "b_router_group": b_router_group,
            "w_router_expert": w_router_expert, "b_router_expert": b_router_expert,
            "w_gate": w_gate, "w_up": w_up, "w_down": w_down,
            "ln2_g": ln2_g, "ln2_b": ln2_b}


def reference(x, w_in, rpb, s5_a_re, s5_a_im, s5_log_dt, s5_b_re, s5_b_im, s5_c_re,
              s5_c_im, s5_d, w_glu, b_glu, w_out, ln1_g, ln1_b, w_router_group,
              b_router_group, w_router_expert, b_router_expert, w_gate, w_up, w_down,
              ln2_g, ln2_b):
    alpha = (2.0 * DEPTH) ** 0.25
    h = x
    for layer in range(DEPTH):
        mix = hybrid_mixer(h, w_in[layer], rpb[layer], s5_a_re[layer], s5_a_im[layer],
                           s5_log_dt[layer], s5_b_re[layer], s5_b_im[layer], s5_c_re[layer],
                           s5_c_im[layer], s5_d[layer], w_glu[layer], b_glu[layer], w_out[layer])
        h = layer_norm(alpha * h + mix, ln1_g[layer], ln1_b[layer])
        ffn = hierarchical_moe(h, w_router_group[layer], b_router_group[layer],
                               w_router_expert[layer], b_router_expert[layer],
                               w_gate[layer], w_up[layer], w_down[layer])
        h = layer_norm(alpha * h + ffn, ln2_g[layer], ln2_b[layer])
    return h
```

```python
import functools
import math

import numpy as np
import jax
import jax.numpy as jnp
from jax import lax
from jax.experimental import pallas as pl
from jax.experimental.pallas import tpu as pltpu

F32 = jnp.float32
BF16 = jnp.bfloat16

GRID_W = 64
WIN_H = 8
WIN_W = 16
HEAD_DIM = 64
N_EXPERT_GROUPS = 8
EXPERTS_PER_GROUP = 8
N_EXPERTS = N_EXPERT_GROUPS * EXPERTS_PER_GROUP
S5_GROUP_CH = 16
S5_STATE = 64
LN_EPS = 1e-5
DEPTH = 1
ALPHA = (2.0 * DEPTH) ** 0.25

LANES = 128
SUBLANES = 8
VMEM_LIMIT_BYTES = 56 * 1024 * 1024

TOKEN_TILE = 512
S5_CHUNK = 16
S5_ROW_TILE = 512
ATTN_ROWS_PER_STEP = 8
EXPERT_BLOCK = 256
NEG_BIAS = -1e30


def _cparams(sem):
    return pltpu.CompilerParams(dimension_semantics=sem,
                                vmem_limit_bytes=VMEM_LIMIT_BYTES)


def _inproj_kernel(x_ref, w_ref, q_ref, k_ref, v_ref, u_ref, *, width, scale):
    x = x_ref[...].astype(BF16)
    outs = (q_ref, k_ref, v_ref, u_ref)
    for i, o_ref in enumerate(outs):
        acc = jnp.dot(x, w_ref[:, i * width:(i + 1) * width],
                      preferred_element_type=F32)
        if i == 0:
            acc = acc * scale
        o_ref[...] = acc.astype(BF16)


def _in_proj(x2d, w_in_bf):
    n_tok, d_model = x2d.shape
    width = w_in_bf.shape[1] // 4
    tm = TOKEN_TILE
    out = jax.ShapeDtypeStruct((n_tok, width), BF16)
    o_spec = pl.BlockSpec((tm, width), lambda i: (i, 0))
    return pl.pallas_call(
        functools.partial(_inproj_kernel, width=width, scale=HEAD_DIM ** -0.5),
        out_shape=(out, out, out, out),
        grid=(n_tok // tm,),
        in_specs=[pl.BlockSpec((tm, d_model), lambda i: (i, 0)),
                  pl.BlockSpec((d_model, 4 * width), lambda i: (0, 0))],
        out_specs=(o_spec, o_spec, o_spec, o_spec),
        compiler_params=_cparams(("parallel",)),
        name="in_proj",
    )(x2d, w_in_bf)


def _attn_bias_table(rpb):
    col = np.arange(GRID_W)
    cstart = np.clip(col - WIN_W // 2, 0, GRID_W - WIN_W)
    j = np.arange(GRID_W)
    valid = (j[None, :] >= cstart[:, None]) & (j[None, :] < cstart[:, None] + WIN_W)
    dcol = np.clip(j[None, :] - col[:, None] + (WIN_W - 1), 0, 2 * WIN_W - 2)
    full = rpb.astype(F32)[:, :, dcol]
    full = jnp.where(jnp.asarray(valid)[None, None], full, NEG_BIAS)
    pair = jnp.concatenate([full[:, :-1], full[:, 1:]], axis=-1)
    return jnp.transpose(pair, (1, 0, 2, 3))


def _attn_kernel(q_ref, k_ref, v_ref, b_ref, o_ref, *, n_rows, n_heads):
    rblk = pl.program_id(1)
    band = WIN_H * GRID_W
    lane = lax.broadcasted_iota(jnp.int32, (GRID_W, LANES), 1)
    low_half = lane < HEAD_DIM

    def row_body(rr, carry):
        r = rblk * ATTN_ROWS_PER_STEP + rr
        rs = jnp.clip(r - WIN_H // 2, 0, n_rows - WIN_H)
        shift = rs - r + (WIN_H - 1)
        q_off = pl.multiple_of(rr * GRID_W, GRID_W)
        k_off = pl.multiple_of(rs * GRID_W, GRID_W)
        for hp in range(n_heads // 2):
            cols = slice(hp * LANES, (hp + 1) * LANES)
            qp = q_ref[pl.ds(q_off, GRID_W), cols]
            kp = k_ref[pl.ds(k_off, band), cols]
            vp = v_ref[pl.ds(k_off, band), cols]
            outs = []
            for e in range(2):
                head = 2 * hp + e
                keep = low_half if e == 0 else jnp.logical_not(low_half)
                qm = jnp.where(keep, qp, jnp.zeros_like(qp))
                s = lax.dot_general(qm, kp, (((1,), (1,)), ((), ())),
                                    preferred_element_type=F32)
                s = jnp.concatenate(
                    [s[:, m * LANES:(m + 1) * LANES] + b_ref[shift + 2 * m, head]
                     for m in range(band // LANES)], axis=-1)
                mx = jnp.max(s, axis=-1, keepdims=True)
                p = jnp.exp(s - mx)
                den = jnp.sum(p, axis=-1, keepdims=True)
                o = jnp.dot(p.astype(BF16), vp, preferred_element_type=F32)
                outs.append(o / den)
            o_ref[pl.ds(q_off, GRID_W), cols] = jnp.where(low_half, outs[0], outs[1]).astype(BF16)
        return carry

    lax.fori_loop(0, ATTN_ROWS_PER_STEP, row_body, 0)


def _attention(q, k, v, bias_tab, n_heads):
    bsz, seq, width = q.shape
    n_rows = seq // GRID_W
    tq = ATTN_ROWS_PER_STEP * GRID_W
    return pl.pallas_call(
        functools.partial(_attn_kernel, n_rows=n_rows, n_heads=n_heads),
        out_shape=jax.ShapeDtypeStruct((bsz, seq, width), BF16),
        grid=(bsz, seq // tq),
        in_specs=[pl.BlockSpec((None, tq, width), lambda b, r: (b, r, 0)),
                  pl.BlockSpec((None, seq, width), lambda b, r: (b, 0, 0)),
                  pl.BlockSpec((None, seq, width), lambda b, r: (b, 0, 0)),
                  pl.BlockSpec(bias_tab.shape, lambda b, r: (0, 0, 0, 0))],
        out_specs=pl.BlockSpec((None, tq, width), lambda b, r: (b, r, 0)),
        compiler_params=_cparams(("parallel", "arbitrary")),
        name="nbr_attention",
    )(q, k, v, bias_tab)


def _s5_tables(a_re, a_im, log_dt, b_re, b_im, c_re, c_im, d_skip):
    hi = lax.Precision.HIGHEST
    L = S5_CHUNK
    n_grp, n_state = a_re.shape[1], a_re.shape[2]
    n_ch = b_re.shape[-1]
    f = lambda t: t.astype(F32)
    ks = jnp.arange(L + 1, dtype=F32)

    t_dirs, w_re, w_im, v_re, v_im, lam_l_re, lam_l_im = [], [], [], [], [], [], []
    for direction in range(2):
        ar, ai = f(a_re[direction]), f(a_im[direction])
        dt = jnp.exp(f(log_dt[direction]))[:, None]
        mag = jnp.exp(ar * dt)
        lr, li = mag * jnp.cos(ai * dt), mag * jnp.sin(ai * dt)
        den = ar * ar + ai * ai
        zr = ((lr - 1.0) * ar + li * ai) / den
        zi = (li * ar - (lr - 1.0) * ai) / den
        br, bi = f(b_re[direction]), f(b_im[direction])
        bbr = zr[..., None] * br - zi[..., None] * bi
        bbi = zr[..., None] * bi + zi[..., None] * br
        cr, ci = f(c_re[direction]), f(c_im[direction])
        pmag = jnp.exp(ks[:, None, None] * (ar * dt)[None])
        pr = pmag * jnp.cos(ks[:, None, None] * (ai * dt)[None])
        pi = pmag * jnp.sin(ks[:, None, None] * (ai * dt)[None])
        lbr = pr[..., None] * bbr[None] - pi[..., None] * bbi[None]
        lbi = pr[..., None] * bbi[None] + pi[..., None] * bbr[None]
        kern = (jnp.einsum('gcp,kgpd->kgcd', cr, lbr, precision=hi)
                - jnp.einsum('gcp,kgpd->kgcd', ci, lbi, precision=hi))
        s_idx = np.arange(L)[:, None]
        t_idx = np.arange(L)[None, :]
        lag = (t_idx - s_idx) if direction == 0 else (s_idx - t_idx)
        tk = kern[np.clip(lag, 0, L)]
        tk = jnp.where(jnp.asarray(lag >= 0)[:, :, None, None, None], tk, 0.0)
        t_dirs.append(jnp.transpose(tk, (2, 0, 4, 1, 3)))
        pw = (L - 1 - np.arange(L)) if direction == 0 else np.arange(L)
        w_re.append(jnp.transpose(lbr[pw], (1, 0, 3, 2)))
        w_im.append(jnp.transpose(lbi[pw], (1, 0, 3, 2)))
        po = (np.arange(L) + 1) if direction == 0 else (L - np.arange(L))
        clr = cr[None] * jnp.transpose(pr[po], (0, 1, 2))[:, :, None, :] \
            - ci[None] * pi[po][:, :, None, :]
        cli = cr[None] * pi[po][:, :, None, :] + ci[None] * pr[po][:, :, None, :]
        v_re.append(jnp.transpose(clr, (1, 3, 0, 2)))
        v_im.append(jnp.transpose(-cli, (1, 3, 0, 2)))
        lam_l_re.append(pr[L])
        lam_l_im.append(pi[L])

    lc = L * n_ch
    eye = jnp.eye(lc, dtype=F32).reshape(L, n_ch, L, n_ch)
    toep = t_dirs[0] + t_dirs[1] + eye[None] * f(d_skip)[:, None, None, None, :]
    toep = toep.reshape(n_grp, lc, lc)
    w_all = jnp.concatenate([w_re[0], w_re[1], w_im[0], w_im[1]], axis=-1)
    m_mat = jnp.concatenate([toep, w_all.reshape(n_grp, lc, 4 * n_state)], axis=-1)
    v_mat = jnp.concatenate([v_re[0], v_re[1], v_im[0], v_im[1]], axis=1)
    v_mat = v_mat.reshape(n_grp, 4 * n_state, lc)
    lam_re = jnp.concatenate([lam_l_re[0], lam_l_re[1]], axis=-1)
    lam_im = jnp.concatenate([lam_l_im[0], lam_l_im[1]], axis=-1)
    bc = lambda t: jnp.broadcast_to(t[:, None, :], (n_grp, SUBLANES, 2 * n_state))
    return m_mat.astype(BF16), v_mat.astype(BF16), bc(lam_re), bc(lam_im)


def _gelu_tanh(x):
    cdf = 0.5 * (1.0 + jnp.tanh(math.sqrt(2.0 / math.pi) * (x + 0.044715 * (x * x * x))))
    return x * cdf


def _s5_kernel(u_ref, m_ref, v_ref, lr_ref, li_ref, o_ref, y_scr, x_scr, h_scr,
               *, n_chunks, n_state):
    n_rows = u_ref.shape[0]
    lc = o_ref.shape[1]
    rt = min(S5_ROW_TILE, n_rows)

    def proj_body(i, carry):
        rows = pl.ds(pl.multiple_of(i * rt, rt), rt)
        xy = jnp.dot(u_ref[rows, :], m_ref[...], preferred_element_type=F32)
        y_scr[rows, :] = xy[:, :lc]
        x_scr[rows, :] = xy[:, lc:]
        return carry

    lax.fori_loop(0, n_rows // rt, proj_body, 0)

    lam_re = lr_ref[...]
    lam_im = li_ref[...]
    two_p = 2 * n_state
    is_fwd = lax.broadcasted_iota(jnp.int32, (SUBLANES, two_p), 1) < n_state

    def scan_body(j, carry):
        h_re, h_im = carry
        rf = pl.ds(pl.multiple_of(j * SUBLANES, SUBLANES), SUBLANES)
        rb = pl.ds(pl.multiple_of((n_chunks - 1 - j) * SUBLANES, SUBLANES), SUBLANES)
        x_re = jnp.where(is_fwd, x_scr[rf, 0:two_p], x_scr[rb, 0:two_p])
        x_im = jnp.where(is_fwd, x_scr[rf, two_p:2 * two_p], x_scr[rb, two_p:2 * two_p])
        h_scr[rf, 0:n_state] = h_re[:, 0:n_state]
        h_scr[rb, n_state:two_p] = h_re[:, n_state:two_p]
        h_scr[rf, two_p:two_p + n_state] = h_im[:, 0:n_state]
        h_scr[rb, two_p + n_state:2 * two_p] = h_im[:, n_state:two_p]
        n_re = lam_re * h_re - lam_im * h_im + x_re
        n_im = lam_re * h_im + lam_im * h_re + x_im
        return n_re, n_im

    zero = jnp.zeros((SUBLANES, two_p), F32)
    lax.fori_loop(0, n_chunks, scan_body, (zero, zero))

    def out_body(i, carry):
        rows = pl.ds(pl.multiple_of(i * rt, rt), rt)
        y = y_scr[rows, :] + jnp.dot(h_scr[rows, :].astype(BF16), v_ref[...],
                                     preferred_element_type=F32)
        o_ref[rows, :] = _gelu_tanh(y).astype(BF16)
        return carry

    lax.fori_loop(0, n_rows // rt, out_body, 0)


def _s5(u_grp, m_mat, v_mat, lam_re, lam_im, n_chunks):
    n_grp, n_rows, lc = u_grp.shape
    four_p = v_mat.shape[1]
    n_state = four_p // 4
    return pl.pallas_call(
        functools.partial(_s5_kernel, n_chunks=n_chunks, n_state=n_state),
        out_shape=jax.ShapeDtypeStruct((n_grp, n_rows, lc), BF16),
        grid=(n_grp,),
        in_specs=[pl.BlockSpec((None, n_rows, lc), lambda g: (g, 0, 0)),
                  pl.BlockSpec((None, lc, lc + four_p), lambda g: (g, 0, 0)),
                  pl.BlockSpec((None, four_p, lc), lambda g: (g, 0, 0)),
                  pl.BlockSpec((None, SUBLANES, 2 * n_state), lambda g: (g, 0, 0)),
                  pl.BlockSpec((None, SUBLANES, 2 * n_state), lambda g: (g, 0, 0))],
        out_specs=pl.BlockSpec((None, n_rows, lc), lambda g: (g, 0, 0)),
        scratch_shapes=[pltpu.VMEM((n_rows, lc), F32),
                        pltpu.VMEM((n_rows, four_p), F32),
                        pltpu.VMEM((n_rows, four_p), F32)],
        compiler_params=_cparams(("parallel",)),
        name="s5_chunked",
    )(u_grp, m_mat, v_mat, lam_re, lam_im)


def _layer_norm(x, g, b):
    mu = jnp.mean(x, axis=-1, keepdims=True)
    xc = x - mu
    var = jnp.mean(xc * xc, axis=-1, keepdims=True)
    return xc * lax.rsqrt(var + LN_EPS) * g + b


def _first_index_of_max(vals, row_id, n):
    mx = jnp.max(vals, axis=0, keepdims=True)
    idx = jnp.min(jnp.where(vals == mx, row_id, n), axis=0, keepdims=True)
    return mx, idx


def _mix_kernel(x_ref, a_ref, s_ref, wglu_ref, bglu_ref, wout_ref, g_ref, b_ref,
                wr_ref, br_ref, h_ref, eid_ref, gate_ref):
    ssm = s_ref[...]
    half = ssm.shape[1]
    z = jnp.dot(ssm, wglu_ref[...], preferred_element_type=F32) + bglu_ref[...]
    glu = ssm.astype(F32) * jax.nn.sigmoid(z)
    mix = (jnp.dot(a_ref[...], wout_ref[0:half, :], preferred_element_type=F32)
           + jnp.dot(glu.astype(BF16), wout_ref[half:, :], preferred_element_type=F32))
    h = _layer_norm(ALPHA * x_ref[...] + mix, g_ref[...], b_ref[...])
    h_ref[...] = h

    logits = lax.dot_general(wr_ref[...], h.astype(BF16), (((1,), (1,)), ((), ())),
                             preferred_element_type=F32) + br_ref[...]
    ng, epg = N_EXPERT_GROUPS, EXPERTS_PER_GROUP
    tm = logits.shape[1]
    row_id = lax.broadcasted_iota(jnp.int32, (ng, tm), 0)
    g_logit = logits[0:ng, :]
    g_max, g_idx = _first_index_of_max(g_logit, row_id, ng)
    g_val = 1.0 / jnp.sum(jnp.exp(g_logit - g_max), axis=0, keepdims=True)
    e_in = jnp.zeros((epg, tm), F32)
    for gi in range(ng):
        e_in = jnp.where(g_idx == gi, logits[ng + gi * epg:ng + (gi + 1) * epg, :], e_in)
    m1, i1 = _first_index_of_max(e_in, row_id, epg)
    rest = jnp.where(row_id == i1, -jnp.inf, e_in)
    m2, i2 = _first_index_of_max(rest, row_id, epg)
    e2 = jnp.exp(m2 - m1)
    w1 = 1.0 / (1.0 + e2)
    w2 = e2 / (1.0 + e2)
    eid_ref[0:1, :] = g_idx * epg + i1
    eid_ref[1:2, :] = g_idx * epg + i2
    gate_ref[0:1, :] = g_val * w1
    gate_ref[1:2, :] = g_val * w2


def _mix(x2d, attn, ssm, w_glu, b_glu, w_out, ln_g, ln_b, w_router, b_router):
    n_tok, d_model = x2d.shape
    half = attn.shape[1]
    tm = TOKEN_TILE
    const = lambda shape: pl.BlockSpec(shape, lambda i: (0,) * len(shape))
    return pl.pallas_call(
        _mix_kernel,
        out_shape=(jax.ShapeDtypeStruct((n_tok, d_model), F32),
                   jax.ShapeDtypeStruct((2, n_tok), jnp.int32),
                   jax.ShapeDtypeStruct((2, n_tok), F32)),
        grid=(n_tok // tm,),
        in_specs=[pl.BlockSpec((tm, d_model), lambda i: (i, 0)),
                  pl.BlockSpec((tm, half), lambda i: (i, 0)),
                  pl.BlockSpec((tm, half), lambda i: (i, 0)),
                  const(w_glu.shape), const(b_glu.shape), const(w_out.shape),
                  const(ln_g.shape), const(ln_b.shape),
                  const(w_router.shape), const(b_router.shape)],
        out_specs=(pl.BlockSpec((tm, d_model), lambda i: (i, 0)),
                   pl.BlockSpec((2, tm), lambda i: (0, i)),
                   pl.BlockSpec((2, tm), lambda i: (0, i))),
        compiler_params=_cparams(("parallel",)),
        name="mix_ln_router",
    )(x2d, attn, ssm, w_glu, b_glu, w_out, ln_g, ln_b, w_router, b_router)


def _dispatch_plan(eid, n_tok):
    bm = EXPERT_BLOCK
    n_assign = 2 * n_tok
    n_blocks = (n_assign + N_EXPERTS * (bm - 1) + bm - 1) // bm
    n_slots = n_blocks * bm
    flat_e = jnp.transpose(eid).reshape(-1)
    order = jnp.argsort(flat_e, stable=True).astype(jnp.int32)
    se = flat_e[order]
    counts = jnp.bincount(flat_e, length=N_EXPERTS).astype(jnp.int32)
    starts = jnp.cumsum(counts) - counts
    padded = (counts + bm - 1) // bm * bm
    pends = jnp.cumsum(padded)
    pstarts = pends - padded
    slot = pstarts[se] + (jnp.arange(n_assign, dtype=jnp.int32) - starts[se])
    tok = order // 2
    dst_valid = (order % 2) * n_tok + tok
    is_pad = jnp.ones((n_slots,), jnp.int32).at[slot].set(0)
    pad_rank = jnp.cumsum(is_pad) - 1
    src_tok = jnp.zeros((n_slots,), jnp.int32).at[slot].set(tok)
    dst_row = (n_assign + pad_rank).astype(jnp.int32).at[slot].set(dst_valid)
    block_start = jnp.arange(n_blocks, dtype=jnp.int32) * bm
    block_expert = jnp.minimum(jnp.searchsorted(pends, block_start, side='right'),
                               N_EXPERTS - 1).astype(jnp.int32)
    n_used = (pends[-1] // bm).astype(jnp.int32).reshape(1)
    return (src_tok.reshape(n_blocks, 1, bm), dst_row.reshape(n_blocks, 1, bm),
            block_expert, n_used, n_blocks)


def _expert_kernel(bexp_ref, nused_ref, src_ref, dst_ref, h_hbm, wg_ref, wu_ref, wd_ref,
                   y_hbm, x_buf, y_buf, wg_bf, wu_bf, wd_bf, sems):
    i = pl.program_id(0)
    bm = x_buf.shape[0]
    used = i < nused_ref[0]

    def row_copy_in(r):
        return pltpu.make_async_copy(h_hbm.at[pl.ds(src_ref[0, r], 1), :],
                                     x_buf.at[pl.ds(r, 1), :], sems.at[0])

    def row_copy_out(r):
        return pltpu.make_async_copy(y_buf.at[pl.ds(r, 1), :],
                                     y_hbm.at[pl.ds(dst_ref[0, r], 1), :], sems.at[1])

    @pl.when(used)
    def _():
        for r in range(bm):
            row_copy_in(r).start()
        prev = bexp_ref[jnp.maximum(i - 1, 0)]

        @pl.when(jnp.logical_or(i == 0, bexp_ref[i] != prev))
        def _():
            wg_bf[...] = wg_ref[...].astype(BF16)
            wu_bf[...] = wu_ref[...].astype(BF16)
            wd_bf[...] = wd_ref[...].astype(BF16)

        for r in range(bm):
            row_copy_in(r).wait()
        x = x_buf[...].astype(BF16)
        gate = jnp.dot(x, wg_bf[...], preferred_element_type=F32)
        up = jnp.dot(x, wu_bf[...], preferred_element_type=F32)
        hid = (jax.nn.silu(gate) * up).astype(BF16)
        y_buf[...] = jnp.dot(hid, wd_bf[...], preferred_element_type=F32)

    @pl.when(jnp.logical_not(used))
    def _():
        y_buf[...] = jnp.zeros_like(y_buf)

    for r in range(bm):
        row_copy_out(r).start()
    for r in range(bm):
        row_copy_out(r).wait()


def _experts(h1, src_tok, dst_row, block_expert, n_used, n_blocks, w_gate, w_up, w_down):
    n_tok, d_model = h1.shape
    d_exp = w_gate.shape[-1]
    bm = EXPERT_BLOCK
    smem_rows = lambda: pl.BlockSpec((None, 1, bm), lambda i, be, nu: (i, 0, 0),
                                     memory_space=pltpu.SMEM)
    grid_spec = pltpu.PrefetchScalarGridSpec(
        num_scalar_prefetch=2,
        grid=(n_blocks,),
        in_specs=[smem_rows(), smem_rows(),
                  pl.BlockSpec(memory_space=pl.ANY),
                  pl.BlockSpec((None, d_model, d_exp), lambda i, be, nu: (be[i], 0, 0)),
                  pl.BlockSpec((None, d_model, d_exp), lambda i, be, nu: (be[i], 0, 0)),
                  pl.BlockSpec((None, d_exp, d_model), lambda i, be, nu: (be[i], 0, 0))],
        out_specs=pl.BlockSpec(memory_space=pl.ANY),
        scratch_shapes=[pltpu.VMEM((bm, d_model), F32),
                        pltpu.VMEM((bm, d_model), F32),
                        pltpu.VMEM((d_model, d_exp), BF16),
                        pltpu.VMEM((d_model, d_exp), BF16),
                        pltpu.VMEM((d_exp, d_model), BF16),
                        pltpu.SemaphoreType.DMA((2,))])
    return pl.pallas_call(
        _expert_kernel,
        out_shape=jax.ShapeDtypeStruct((n_blocks * bm, d_model), F32),
        grid_spec=grid_spec,
        compiler_params=_cparams(("arbitrary",)),
        name="expert_ffn",
    )(block_expert, n_used, src_tok, dst_row, h1, w_gate, w_up, w_down)


def _final_kernel(h_ref, y0_ref, y1_ref, gate_ref, g_ref, b_ref, o_ref):
    gate = gate_ref[...]
    y = y0_ref[...] * gate[:, 0:1] + y1_ref[...] * gate[:, 1:2]
    o_ref[...] = _layer_norm(ALPHA * h_ref[...] + y, g_ref[...], b_ref[...])


def _final(h1, ybuf, gate_t, ln_g, ln_b):
    n_tok, d_model = h1.shape
    tm = TOKEN_TILE
    k1_off = n_tok // tm
    const = lambda shape: pl.BlockSpec(shape, lambda i: (0,) * len(shape))
    return pl.pallas_call(
        _final_kernel,
        out_shape=jax.ShapeDtypeStruct((n_tok, d_model), F32),
        grid=(n_tok // tm,),
        in_specs=[pl.BlockSpec((tm, d_model), lambda i: (i, 0)),
                  pl.BlockSpec((tm, d_model), lambda i: (i, 0)),
                  pl.BlockSpec((tm, d_model), lambda i: (i + k1_off, 0)),
                  pl.BlockSpec((tm, 2), lambda i: (i, 0)),
                  const(ln_g.shape), const(ln_b.shape)],
        out_specs=pl.BlockSpec((tm, d_model), lambda i: (i, 0)),
        compiler_params=_cparams(("parallel",)),
        name="combine_ln",
    )(h1, ybuf, ybuf, gate_t, ln_g, ln_b)


def kernel(x, w_in, rpb, s5_a_re, s5_a_im, s5_log_dt, s5_b_re, s5_b_im, s5_c_re, s5_c_im, s5_d, w_glu, b_glu, w_out, ln1_g, ln1_b, w_router_group, b_router_group, w_router_expert, b_router_expert, w_gate, w_up, w_down, ln2_g, ln2_b):
    bsz, seq, d_model = x.shape
    n_tok = bsz * seq
    assert bsz == SUBLANES, "the S5 chunk scan keeps one batch element per sublane"
    assert seq % (ATTN_ROWS_PER_STEP * GRID_W) == 0 and n_tok % TOKEN_TILE == 0
    assert w_in.shape[0] == DEPTH
    layer = 0
    x2d = x.reshape(n_tok, d_model)
    width = w_in.shape[-1] // 4
    n_heads = width // HEAD_DIM
    n_grp = s5_d.shape[1]

    q, k, v, u = _in_proj(x2d, w_in[layer].astype(BF16))

    shape3 = (bsz, seq, width)
    attn = _attention(q.reshape(shape3), k.reshape(shape3), v.reshape(shape3),
                      _attn_bias_table(rpb[layer]), n_heads).reshape(n_tok, width)

    n_chunks = seq // S5_CHUNK
    u_grp = u.reshape(bsz, n_chunks, S5_CHUNK, n_grp, S5_GROUP_CH)
    u_grp = jnp.transpose(u_grp, (3, 1, 0, 2, 4)).reshape(n_grp, n_chunks * bsz,
                                                          S5_CHUNK * S5_GROUP_CH)
    m_mat, v_mat, lam_re, lam_im = _s5_tables(
        s5_a_re[layer], s5_a_im[layer], s5_log_dt[layer], s5_b_re[layer], s5_b_im[layer],
        s5_c_re[layer], s5_c_im[layer], s5_d[layer])
    ssm = _s5(u_grp, m_mat, v_mat, lam_re, lam_im, n_chunks)
    ssm = ssm.reshape(n_grp, n_chunks, bsz, S5_CHUNK, S5_GROUP_CH)
    ssm = jnp.transpose(ssm, (2, 1, 3, 0, 4)).reshape(n_tok, width)

    n_router = N_EXPERT_GROUPS + N_EXPERTS
    w_router = jnp.concatenate([w_router_group[layer], w_router_expert[layer]], axis=1)
    w_router = jnp.pad(jnp.transpose(w_router), ((0, LANES - n_router), (0, 0))).astype(BF16)
    b_router = jnp.concatenate([b_router_group[layer], b_router_expert[layer]])
    b_router = jnp.pad(b_router, (0, LANES - n_router)).astype(F32).reshape(LANES, 1)
    row = lambda t: t.astype(F32).reshape(1, -1)
    h1, eid, gate = _mix(x2d, attn, ssm, w_glu[layer].astype(BF16), row(b_glu[layer]),
                         w_out[layer].astype(BF16), row(ln1_g[layer]), row(ln1_b[layer]),
                         w_router, b_router)

    src_tok, dst_row, block_expert, n_used, n_blocks = _dispatch_plan(eid, n_tok)
    ybuf = _experts(h1, src_tok, dst_row, block_expert, n_used, n_blocks,
                    w_gate[layer], w_up[layer], w_down[layer])

    out = _final(h1, ybuf, jnp.transpose(gate), row(ln2_g[layer]), row(ln2_b[layer]))
    return out.reshape(bsz, seq, d_model)
```

```python
import functools
import math

import numpy as np
import jax
import jax.numpy as jnp
from jax import lax
from jax.experimental import pallas as pl
from jax.experimental.pallas import tpu as pltpu

F32 = jnp.float32
BF16 = jnp.bfloat16
U32 = jnp.uint32

GRID_W = 64
WIN_H = 8
WIN_W = 16
HEAD_DIM = 64
N_EXPERT_GROUPS = 8
EXPERTS_PER_GROUP = 8
N_EXPERTS = N_EXPERT_GROUPS * EXPERTS_PER_GROUP
TOP_K = 2
S5_GROUP_CH = 16
S5_STATE = 64
LN_EPS = 1e-5
DEPTH = 1
ALPHA = (2.0 * DEPTH) ** 0.25

LANES = 128
SUBLANES = 8
VMEM_LIMIT_BYTES = 56 * 1024 * 1024

TOKEN_TILE = 512
S5_CHUNK = 16
S5_ROW_TILE = 512
ATTN_ROWS_PER_STEP = 8
EXPERT_BLOCK = 256
NEG_BIAS = -1e30


def _cparams(sem):
    return pltpu.CompilerParams(dimension_semantics=sem,
                                vmem_limit_bytes=VMEM_LIMIT_BYTES)


def _inproj_kernel(x_ref, w_ref, q_ref, k_ref, v_ref, u_ref, *, width, scale):
    x = x_ref[...].astype(BF16)
    outs = (q_ref, k_ref, v_ref, u_ref)
    for i, o_ref in enumerate(outs):
        acc = jnp.dot(x, w_ref[:, i * width:(i + 1) * width],
                      preferred_element_type=F32)
        if i == 0:
            acc = acc * scale
        o_ref[...] = acc.astype(BF16)


def _in_proj(x2d, w_in_bf):
    n_tok, d_model = x2d.shape
    width = w_in_bf.shape[1] // 4
    tm = TOKEN_TILE
    out = jax.ShapeDtypeStruct((n_tok, width), BF16)
    o_spec = pl.BlockSpec((tm, width), lambda i: (i, 0))
    return pl.pallas_call(
        functools.partial(_inproj_kernel, width=width, scale=HEAD_DIM ** -0.5),
        out_shape=(out, out, out, out),
        grid=(n_tok // tm,),
        in_specs=[pl.BlockSpec((tm, d_model), lambda i: (i, 0)),
                  pl.BlockSpec((d_model, 4 * width), lambda i: (0, 0))],
        out_specs=(o_spec, o_spec, o_spec, o_spec),
        compiler_params=_cparams(("parallel",)),
        name="in_proj",
    )(x2d, w_in_bf)


def _attn_bias_table(rpb):
    col = np.arange(GRID_W)
    cstart = np.clip(col - WIN_W // 2, 0, GRID_W - WIN_W)
    j = np.arange(GRID_W)
    valid = (j[None, :] >= cstart[:, None]) & (j[None, :] < cstart[:, None] + WIN_W)
    dcol = np.clip(j[None, :] - col[:, None] + (WIN_W - 1), 0, 2 * WIN_W - 2)
    full = rpb.astype(F32)[:, :, dcol]
    full = jnp.where(jnp.asarray(valid)[None, None], full, NEG_BIAS)
    pair = jnp.concatenate([full[:, :-1], full[:, 1:]], axis=-1)
    return jnp.transpose(pair, (1, 0, 2, 3))


def _attn_kernel(q_ref, k_ref, v_ref, b_ref, o_ref, *, n_rows, n_heads):
    rblk = pl.program_id(1)
    band = WIN_H * GRID_W
    lane = lax.broadcasted_iota(jnp.int32, (GRID_W, LANES), 1)
    low_half = lane < HEAD_DIM

    def row_body(rr, carry):
        r = rblk * ATTN_ROWS_PER_STEP + rr
        rs = jnp.clip(r - WIN_H // 2, 0, n_rows - WIN_H)
        shift = rs - r + (WIN_H - 1)
        q_off = pl.multiple_of(rr * GRID_W, GRID_W)
        k_off = pl.multiple_of(rs * GRID_W, GRID_W)
        for hp in range(n_heads // 2):
            cols = slice(hp * LANES, (hp + 1) * LANES)
            qp = q_ref[pl.ds(q_off, GRID_W), cols]
            kp = k_ref[pl.ds(k_off, band), cols]
            vp = v_ref[pl.ds(k_off, band), cols]
            outs = []
            for e in range(2):
                head = 2 * hp + e
                keep = low_half if e == 0 else jnp.logical_not(low_half)
                qm = jnp.where(keep, qp, jnp.zeros_like(qp))
                s = lax.dot_general(qm, kp, (((1,), (1,)), ((), ())),
                                    preferred_element_type=F32)
                s = jnp.concatenate(
                    [s[:, m * LANES:(m + 1) * LANES] + b_ref[shift + 2 * m, head]
                     for m in range(band // LANES)], axis=-1)
                mx = jnp.max(s, axis=-1, keepdims=True)
                p = jnp.exp(s - mx)
                den = jnp.sum(p, axis=-1, keepdims=True)
                o = jnp.dot(p.astype(BF16), vp, preferred_element_type=F32)
                outs.append(o / den)
            o_ref[pl.ds(q_off, GRID_W), cols] = jnp.where(low_half, outs[0], outs[1]).astype(BF16)
        return carry

    lax.fori_loop(0, ATTN_ROWS_PER_STEP, row_body, 0)


def _attention(q, k, v, bias_tab, n_heads):
    bsz, seq, width = q.shape
    n_rows = seq // GRID_W
    tq = ATTN_ROWS_PER_STEP * GRID_W
    return pl.pallas_call(
        functools.partial(_attn_kernel, n_rows=n_rows, n_heads=n_heads),
        out_shape=jax.ShapeDtypeStruct((bsz, seq, width), BF16),
        grid=(bsz, seq // tq),
        in_specs=[pl.BlockSpec((None, tq, width), lambda b, r: (b, r, 0)),
                  pl.BlockSpec((None, seq, width), lambda b, r: (b, 0, 0)),
                  pl.BlockSpec((None, seq, width), lambda b, r: (b, 0, 0)),
                  pl.BlockSpec(bias_tab.shape, lambda b, r: (0, 0, 0, 0))],
        out_specs=pl.BlockSpec((None, tq, width), lambda b, r: (b, r, 0)),
        compiler_params=_cparams(("parallel", "arbitrary")),
        name="nbr_attention",
    )(q, k, v, bias_tab)


def _s5_tables(a_re, a_im, log_dt, b_re, b_im, c_re, c_im, d_skip):
    hi = lax.Precision.HIGHEST
    L = S5_CHUNK
    n_grp, n_state = a_re.shape[1], a_re.shape[2]
    n_ch = b_re.shape[-1]
    f = lambda t: t.astype(F32)
    ks = jnp.arange(L + 1, dtype=F32)

    t_dirs, w_re, w_im, v_re, v_im, lam_l_re, lam_l_im = [], [], [], [], [], [], []
    for direction in range(2):
        ar, ai = f(a_re[direction]), f(a_im[direction])
        dt = jnp.exp(f(log_dt[direction]))[:, None]
        mag = jnp.exp(ar * dt)
        lr, li = mag * jnp.cos(ai * dt), mag * jnp.sin(ai * dt)
        den = ar * ar + ai * ai
        zr = ((lr - 1.0) * ar + li * ai) / den
        zi = (li * ar - (lr - 1.0) * ai) / den
        br, bi = f(b_re[direction]), f(b_im[direction])
        bbr = zr[..., None] * br - zi[..., None] * bi
        bbi = zr[..., None] * bi + zi[..., None] * br
        cr, ci = f(c_re[direction]), f(c_im[direction])
        pmag = jnp.exp(ks[:, None, None] * (ar * dt)[None])
        pr = pmag * jnp.cos(ks[:, None, None] * (ai * dt)[None])
        pi = pmag * jnp.sin(ks[:, None, None] * (ai * dt)[None])
        lbr = pr[..., None] * bbr[None] - pi[..., None] * bbi[None]
        lbi = pr[..., None] * bbi[None] + pi[..., None] * bbr[None]
        kern = (jnp.einsum('gcp,kgpd->kgcd', cr, lbr, precision=hi)
                - jnp.einsum('gcp,kgpd->kgcd', ci, lbi, precision=hi))
        s_idx = np.arange(L)[:, None]
        t_idx = np.arange(L)[None, :]
        lag = (t_idx - s_idx) if direction == 0 else (s_idx - t_idx)
        tk = kern[np.clip(lag, 0, L)]
        tk = jnp.where(jnp.asarray(lag >= 0)[:, :, None, None, None], tk, 0.0)
        t_dirs.append(jnp.transpose(tk, (2, 0, 4, 1, 3)))
        pw = (L - 1 - np.arange(L)) if direction == 0 else np.arange(L)
        w_re.append(jnp.transpose(lbr[pw], (1, 0, 3, 2)))
        w_im.append(jnp.transpose(lbi[pw], (1, 0, 3, 2)))
        po = (np.arange(L) + 1) if direction == 0 else (L - np.arange(L))
        clr = cr[None] * pr[po][:, :, None, :] - ci[None] * pi[po][:, :, None, :]
        cli = cr[None] * pi[po][:, :, None, :] + ci[None] * pr[po][:, :, None, :]
        v_re.append(jnp.transpose(clr, (1, 3, 0, 2)))
        v_im.append(jnp.transpose(-cli, (1, 3, 0, 2)))
        lam_l_re.append(pr[L])
        lam_l_im.append(pi[L])

    lc = L * n_ch
    eye = jnp.eye(lc, dtype=F32).reshape(L, n_ch, L, n_ch)
    toep = t_dirs[0] + t_dirs[1] + eye[None] * f(d_skip)[:, None, None, None, :]
    toep = toep.reshape(n_grp, lc, lc)
    w_all = jnp.concatenate([w_re[0], w_re[1], w_im[0], w_im[1]], axis=-1)
    m_mat = jnp.concatenate([toep, w_all.reshape(n_grp, lc, 4 * n_state)], axis=-1)
    v_mat = jnp.concatenate([v_re[0], v_re[1], v_im[0], v_im[1]], axis=1)
    v_mat = v_mat.reshape(n_grp, 4 * n_state, lc)
    lam_re = jnp.concatenate([lam_l_re[0], lam_l_re[1]], axis=-1)
    lam_im = jnp.concatenate([lam_l_im[0], lam_l_im[1]], axis=-1)
    bc = lambda t: jnp.broadcast_to(t[:, None, :], (n_grp, SUBLANES, 2 * n_state))
    return m_mat.astype(BF16), v_mat.astype(BF16), bc(lam_re), bc(lam_im)


def _gelu_tanh(x):
    cdf = 0.5 * (1.0 + jnp.tanh(math.sqrt(2.0 / math.pi) * (x + 0.044715 * (x * x * x))))
    return x * cdf


def _s5_kernel(u_ref, m_ref, v_ref, lr_ref, li_ref, o_ref, y_scr, x_scr, h_scr,
               *, n_chunks, n_state):
    n_rows = u_ref.shape[0]
    lc = o_ref.shape[1]
    rt = min(S5_ROW_TILE, n_rows)

    def proj_body(i, carry):
        rows = pl.ds(pl.multiple_of(i * rt, rt), rt)
        xy = jnp.dot(u_ref[rows, :], m_ref[...], preferred_element_type=F32)
        y_scr[rows, :] = xy[:, :lc]
        x_scr[rows, :] = xy[:, lc:]
        return carry

    lax.fori_loop(0, n_rows // rt, proj_body, 0)

    lam_re = lr_ref[...]
    lam_im = li_ref[...]
    two_p = 2 * n_state
    is_fwd = lax.broadcasted_iota(jnp.int32, (SUBLANES, two_p), 1) < n_state

    def scan_body(j, carry):
        h_re, h_im = carry
        rf = pl.ds(pl.multiple_of(j * SUBLANES, SUBLANES), SUBLANES)
        rb = pl.ds(pl.multiple_of((n_chunks - 1 - j) * SUBLANES, SUBLANES), SUBLANES)
        x_re = jnp.where(is_fwd, x_scr[rf, 0:two_p], x_scr[rb, 0:two_p])
        x_im = jnp.where(is_fwd, x_scr[rf, two_p:2 * two_p], x_scr[rb, two_p:2 * two_p])
        h_scr[rf, 0:n_state] = h_re[:, 0:n_state]
        h_scr[rb, n_state:two_p] = h_re[:, n_state:two_p]
        h_scr[rf, two_p:two_p + n_state] = h_im[:, 0:n_state]
        h_scr[rb, two_p + n_state:2 * two_p] = h_im[:, n_state:two_p]
        n_re = lam_re * h_re - lam_im * h_im + x_re
        n_im = lam_re * h_im + lam_im * h_re + x_im
        return n_re, n_im

    zero = jnp.zeros((SUBLANES, two_p), F32)
    lax.fori_loop(0, n_chunks, scan_body, (zero, zero))

    def out_body(i, carry):
        rows = pl.ds(pl.multiple_of(i * rt, rt), rt)
        y = y_scr[rows, :] + jnp.dot(h_scr[rows, :].astype(BF16), v_ref[...],
                                     preferred_element_type=F32)
        o_ref[rows, :] = _gelu_tanh(y).astype(BF16)
        return carry

    lax.fori_loop(0, n_rows // rt, out_body, 0)


def _s5(u_grp, m_mat, v_mat, lam_re, lam_im, n_chunks):
    n_grp, n_rows, lc = u_grp.shape
    four_p = v_mat.shape[1]
    n_state = four_p // 4
    return pl.pallas_call(
        functools.partial(_s5_kernel, n_chunks=n_chunks, n_state=n_state),
        out_shape=jax.ShapeDtypeStruct((n_grp, n_rows, lc), BF16),
        grid=(n_grp,),
        in_specs=[pl.BlockSpec((None, n_rows, lc), lambda g: (g, 0, 0)),
                  pl.BlockSpec((None, lc, lc + four_p), lambda g: (g, 0, 0)),
                  pl.BlockSpec((None, four_p, lc), lambda g: (g, 0, 0)),
                  pl.BlockSpec((None, SUBLANES, 2 * n_state), lambda g: (g, 0, 0)),
                  pl.BlockSpec((None, SUBLANES, 2 * n_state), lambda g: (g, 0, 0))],
        out_specs=pl.BlockSpec((None, n_rows, lc), lambda g: (g, 0, 0)),
        scratch_shapes=[pltpu.VMEM((n_rows, lc), F32),
                        pltpu.VMEM((n_rows, four_p), F32),
                        pltpu.VMEM((n_rows, four_p), F32)],
        compiler_params=_cparams(("parallel",)),
        name="s5_chunked",
    )(u_grp, m_mat, v_mat, lam_re, lam_im)


def _layer_norm(x, g, b):
    mu = jnp.mean(x, axis=-1, keepdims=True)
    xc = x - mu
    var = jnp.mean(xc * xc, axis=-1, keepdims=True)
    return xc * lax.rsqrt(var + LN_EPS) * g + b


def _pack_bf16_pair(lo, hi):
    lo_bits = lax.bitcast_convert_type(lo.astype(BF16).astype(F32), U32) >> 16
    hi_bits = lax.bitcast_convert_type(hi.astype(BF16).astype(F32), U32) & jnp.uint32(0xFFFF0000)
    return lo_bits | hi_bits


def _unpack_bf16_pair(word):
    lo = lax.bitcast_convert_type(word << 16, F32)
    hi = lax.bitcast_convert_type(word & jnp.uint32(0xFFFF0000), F32)
    return lo, hi


def _first_index_of_max(vals, row_id, n):
    mx = jnp.max(vals, axis=0, keepdims=True)
    idx = jnp.min(jnp.where(vals == mx, row_id, n), axis=0, keepdims=True)
    return mx, idx


def _mix_kernel(x_ref, a_ref, s_ref, wglu_ref, bglu_ref, wout_ref, g_ref, b_ref,
                wr_ref, br_ref, tri_ref, h_ref, hp_ref, eid_ref, gate_ref, rank_ref, hist_ref):
    ssm = s_ref[...]
    half = ssm.shape[1]
    z = jnp.dot(ssm, wglu_ref[...], preferred_element_type=F32) + bglu_ref[...]
    glu = ssm.astype(F32) * jax.nn.sigmoid(z)
    mix = (jnp.dot(a_ref[...], wout_ref[0:half, :], preferred_element_type=F32)
           + jnp.dot(glu.astype(BF16), wout_ref[half:, :], preferred_element_type=F32))
    h = _layer_norm(ALPHA * x_ref[...] + mix, g_ref[...], b_ref[...])
    h_ref[...] = h
    d_half = h.shape[1] // 2
    hp_ref[...] = _pack_bf16_pair(h[:, :d_half], h[:, d_half:])

    logits = lax.dot_general(wr_ref[...], h.astype(BF16), (((1,), (1,)), ((), ())),
                             preferred_element_type=F32) + br_ref[...]
    ng, epg = N_EXPERT_GROUPS, EXPERTS_PER_GROUP
    tm = logits.shape[1]
    row_id = lax.broadcasted_iota(jnp.int32, (ng, tm), 0)
    g_logit = logits[0:ng, :]
    g_max, g_idx = _first_index_of_max(g_logit, row_id, ng)
    g_val = 1.0 / jnp.sum(jnp.exp(g_logit - g_max), axis=0, keepdims=True)
    e_in = jnp.zeros((epg, tm), F32)
    for gi in range(ng):
        e_in = jnp.where(g_idx == gi, logits[ng + gi * epg:ng + (gi + 1) * epg, :], e_in)
    m1, i1 = _first_index_of_max(e_in, row_id, epg)
    rest = jnp.where(row_id == i1, -jnp.inf, e_in)
    m2, i2 = _first_index_of_max(rest, row_id, epg)
    e2 = jnp.exp(m2 - m1)
    w1 = 1.0 / (1.0 + e2)
    w2 = e2 / (1.0 + e2)
    eids = (g_idx * epg + i1, g_idx * epg + i2)
    gate_ref[0:1, :] = g_val * w1
    gate_ref[1:2, :] = g_val * w2

    exp_id = lax.broadcasted_iota(jnp.int32, (N_EXPERTS, tm), 0)
    before = jnp.zeros((N_EXPERTS, 1), F32)
    for kk in range(TOP_K):
        onehot = exp_id == eids[kk]
        prefix = jnp.dot(onehot.astype(BF16), tri_ref[...], preferred_element_type=F32)
        rank = jnp.sum(jnp.where(onehot, prefix + before, 0.0), axis=0, keepdims=True)
        eid_ref[kk:kk + 1, :] = eids[kk]
        rank_ref[kk:kk + 1, :] = rank.astype(jnp.int32)
        before = before + jnp.sum(onehot.astype(F32), axis=1, keepdims=True)
    hist_ref[...] = jnp.broadcast_to(before, hist_ref.shape).astype(jnp.int32)


def _mix(x2d, attn, ssm, w_glu, b_glu, w_out, ln_g, ln_b, w_router, b_router):
    n_tok, d_model = x2d.shape
    half = attn.shape[1]
    tm = TOKEN_TILE
    n_tiles = n_tok // tm
    tri = jnp.asarray(np.triu(np.ones((tm, tm), np.float32), k=1), dtype=BF16)
    const = lambda shape: pl.BlockSpec(shape, lambda i: (0,) * len(shape))
    sel = lambda: pl.BlockSpec((TOP_K, tm), lambda i: (0, i))
    return pl.pallas_call(
        _mix_kernel,
        out_shape=(jax.ShapeDtypeStruct((n_tok, d_model), F32),
                   jax.ShapeDtypeStruct((n_tok, d_model // 2), U32),
                   jax.ShapeDtypeStruct((TOP_K, n_tok), jnp.int32),
                   jax.ShapeDtypeStruct((TOP_K, n_tok), F32),
                   jax.ShapeDtypeStruct((TOP_K, n_tok), jnp.int32),
                   jax.ShapeDtypeStruct((n_tiles, N_EXPERTS, LANES), jnp.int32)),
        grid=(n_tiles,),
        in_specs=[pl.BlockSpec((tm, d_model), lambda i: (i, 0)),
                  pl.BlockSpec((tm, half), lambda i: (i, 0)),
                  pl.BlockSpec((tm, half), lambda i: (i, 0)),
                  const(w_glu.shape), const(b_glu.shape), const(w_out.shape),
                  const(ln_g.shape), const(ln_b.shape),
                  const(w_router.shape), const(b_router.shape), const(tri.shape)],
        out_specs=(pl.BlockSpec((tm, d_model), lambda i: (i, 0)),
                   pl.BlockSpec((tm, d_model // 2), lambda i: (i, 0)),
                   sel(), sel(), sel(),
                   pl.BlockSpec((None, N_EXPERTS, LANES), lambda i: (i, 0, 0))),
        compiler_params=_cparams(("parallel",)),
        name="mix_ln_router",
    )(x2d, attn, ssm, w_glu, b_glu, w_out, ln_g, ln_b, w_router, b_router, tri)


def _dispatch_plan(eid, rank, hist, n_tok):
    bm = EXPERT_BLOCK
    n_assign = TOP_K * n_tok
    tm = TOKEN_TILE
    n_tiles = n_tok // tm
    counts = jnp.sum(hist, axis=0)
    ends = jnp.cumsum(counts)
    off = ends - counts
    base = off[None, :] + jnp.cumsum(hist, axis=0) - hist
    eid_t = eid.reshape(TOP_K, n_tiles, tm)
    onehot = eid_t[..., None] == jnp.arange(N_EXPERTS, dtype=jnp.int32)
    slot = rank.reshape(TOP_K, n_tiles, tm) + jnp.sum(
        jnp.where(onehot, base[None, :, None, :], 0), axis=-1)
    slot = slot.astype(jnp.int32)

    n_mtiles = n_assign // bm
    n_work = n_mtiles + N_EXPERTS
    t_lo = off // bm
    n_vis = jnp.where(counts > 0, (ends + bm - 1) // bm - t_lo, 0)
    w_end = jnp.cumsum(n_vis)
    w_start = w_end - n_vis
    n_valid = w_end[-1]
    w = jnp.arange(n_work, dtype=jnp.int32)
    wc = jnp.minimum(w, n_valid - 1)
    e_w = jnp.minimum(jnp.searchsorted(w_end, wc, side='right'), N_EXPERTS - 1).astype(jnp.int32)
    tile_w = (t_lo[e_w] + (wc - w_start[e_w])).astype(jnp.int32)
    valid = w < n_valid
    lo_w = jnp.where(valid, off[e_w], 0).astype(jnp.int32)
    hi_w = jnp.where(valid, ends[e_w], 0).astype(jnp.int32)
    prev_tile = jnp.concatenate([jnp.full((1,), -1, jnp.int32), tile_w[:-1]])
    first_w = (tile_w != prev_tile).astype(jnp.int32)
    prev_e = jnp.concatenate([jnp.full((1,), -1, jnp.int32), e_w[:-1]])
    newexp_w = (e_w != prev_e).astype(jnp.int32)
    work = dict(tile=tile_w, expert=e_w, lo=lo_w, hi=hi_w, first=first_w, newexp=newexp_w,
                valid=valid.astype(jnp.int32))
    return slot, work, n_work


def _dispatch_kernel(s0_ref, s1_ref, hp_ref, xs_hbm, sem):
    tm = hp_ref.shape[0]

    def row_copy(r, s_ref):
        return pltpu.make_async_copy(hp_ref.at[pl.ds(r, 1), :],
                                     xs_hbm.at[pl.ds(s_ref[0, r], 1), :], sem.at[0])

    for r in range(tm):
        row_copy(r, s0_ref).start()
        row_copy(r, s1_ref).start()
    for r in range(tm):
        row_copy(r, s0_ref).wait()
        row_copy(r, s1_ref).wait()


def _dispatch(hpack, slot3):
    n_tok, words = hpack.shape
    tm = TOKEN_TILE
    n_tiles = n_tok // tm
    smem = lambda k: pl.BlockSpec((None, 1, tm), lambda i: (i + k * n_tiles, 0, 0),
                                  memory_space=pltpu.SMEM)
    return pl.pallas_call(
        _dispatch_kernel,
        out_shape=jax.ShapeDtypeStruct((TOP_K * n_tok, words), U32),
        grid=(n_tiles,),
        in_specs=[smem(0), smem(1), pl.BlockSpec((tm, words), lambda i: (i, 0))],
        out_specs=pl.BlockSpec(memory_space=pl.ANY),
        scratch_shapes=[pltpu.SemaphoreType.DMA((1,))],
        compiler_params=_cparams(("arbitrary",)),
        name="moe_dispatch",
    )(slot3, slot3, hpack)


def _expert_kernel(tile_ref, exp_ref, lo_ref, hi_ref, first_ref, newexp_ref, valid_ref,
                   x_ref, wg_ref, wu_ref, wd_ref, o_ref, wg_bf, wu_bf, wd_bf):
    w = pl.program_id(0)
    bm, words = x_ref.shape

    @pl.when(valid_ref[w] == 1)
    def _():
        @pl.when(newexp_ref[w] == 1)
        def _():
            wg_bf[...] = wg_ref[...].astype(BF16)
            wu_bf[...] = wu_ref[...].astype(BF16)
            wd_bf[...] = wd_ref[...].astype(BF16)

        x_lo, x_hi = _unpack_bf16_pair(x_ref[...])
        x_lo, x_hi = x_lo.astype(BF16), x_hi.astype(BF16)
        gate = (jnp.dot(x_lo, wg_bf[0:words, :], preferred_element_type=F32)
                + jnp.dot(x_hi, wg_bf[words:, :], preferred_element_type=F32))
        up = (jnp.dot(x_lo, wu_bf[0:words, :], preferred_element_type=F32)
              + jnp.dot(x_hi, wu_bf[words:, :], preferred_element_type=F32))
        hid = (jax.nn.silu(gate) * up).astype(BF16)
        y = jnp.dot(hid, wd_bf[...], preferred_element_type=F32)
        packed = _pack_bf16_pair(y[:, :words], y[:, words:])
        row = tile_ref[w] * bm + lax.broadcasted_iota(jnp.int32, (bm, words), 0)
        mine = jnp.logical_and(row >= lo_ref[w], row < hi_ref[w])

        @pl.when(first_ref[w] == 1)
        def _():
            o_ref[...] = jnp.where(mine, packed, jnp.zeros_like(packed))

        @pl.when(first_ref[w] == 0)
        def _():
            o_ref[...] = jnp.where(mine, packed, o_ref[...])


def _experts(xs, work, n_work, w_gate, w_up, w_down):
    n_rows, words = xs.shape
    d_model, d_exp = w_gate.shape[1], w_gate.shape[2]
    bm = EXPERT_BLOCK
    names = ("tile", "expert", "lo", "hi", "first", "newexp", "valid")
    n_pre = len(names)
    row_map = lambda w, *pre: (pre[0][w], 0)
    exp_map = lambda w, *pre: (pre[1][w], 0, 0)
    grid_spec = pltpu.PrefetchScalarGridSpec(
        num_scalar_prefetch=n_pre,
        grid=(n_work,),
        in_specs=[pl.BlockSpec((bm, words), row_map),
                  pl.BlockSpec((None, d_model, d_exp), exp_map),
                  pl.BlockSpec((None, d_model, d_exp), exp_map),
                  pl.BlockSpec((None, d_exp, d_model), exp_map)],
        out_specs=pl.BlockSpec((bm, words), row_map),
        scratch_shapes=[pltpu.VMEM((d_model, d_exp), BF16),
                        pltpu.VMEM((d_model, d_exp), BF16),
                        pltpu.VMEM((d_exp, d_model), BF16)])
    return pl.pallas_call(
        _expert_kernel,
        out_shape=jax.ShapeDtypeStruct((n_rows, words), U32),
        grid_spec=grid_spec,
        compiler_params=_cparams(("arbitrary",)),
        name="expert_ffn",
    )(*[work[n] for n in names], xs, w_gate, w_up, w_down)


def _final_kernel(s0_ref, s1_ref, n0_ref, n1_ref, h_ref, gate_ref, g_ref, b_ref, ys_hbm,
                  o_ref, buf, sem):
    i = pl.program_id(0)
    n_steps = pl.num_programs(0)
    tm = h_ref.shape[0]
    cur = lax.rem(i, 2)

    def row_copy(r, k, s_ref, which):
        return pltpu.make_async_copy(ys_hbm.at[pl.ds(s_ref[0, r], 1), :],
                                     buf.at[which, k, pl.ds(r, 1), :], sem.at[which])

    def start_tile(refs, which):
        for r in range(tm):
            for k in range(TOP_K):
                row_copy(r, k, refs[k], which).start()

    @pl.when(i == 0)
    def _():
        start_tile((s0_ref, s1_ref), 0)

    @pl.when(i + 1 < n_steps)
    def _():
        start_tile((n0_ref, n1_ref), 1 - cur)

    for r in range(tm):
        for k in range(TOP_K):
            row_copy(r, k, (s0_ref, s1_ref)[k], cur).wait()

    gate = gate_ref[...]
    y0_lo, y0_hi = _unpack_bf16_pair(buf[cur, 0])
    y1_lo, y1_hi = _unpack_bf16_pair(buf[cur, 1])
    g0, g1 = gate[:, 0:1], gate[:, 1:2]
    y = jnp.concatenate([y0_lo * g0 + y1_lo * g1, y0_hi * g0 + y1_hi * g1], axis=-1)
    o_ref[...] = _layer_norm(ALPHA * h_ref[...] + y, g_ref[...], b_ref[...])


def _final(h1, ys, slot3, gate_t, ln_g, ln_b):
    n_tok, d_model = h1.shape
    words = ys.shape[1]
    tm = TOKEN_TILE
    n_tiles = n_tok // tm
    const = lambda shape: pl.BlockSpec(shape, lambda i: (0,) * len(shape))
    cur = lambda k: pl.BlockSpec((None, 1, tm), lambda i: (i + k * n_tiles, 0, 0),
                                 memory_space=pltpu.SMEM)
    nxt = lambda k: pl.BlockSpec(
        (None, 1, tm), lambda i: (jnp.minimum(i + 1, n_tiles - 1) + k * n_tiles, 0, 0),
        memory_space=pltpu.SMEM)
    return pl.pallas_call(
        _final_kernel,
        out_shape=jax.ShapeDtypeStruct((n_tok, d_model), F32),
        grid=(n_tiles,),
        in_specs=[cur(0), cur(1), nxt(0), nxt(1),
                  pl.BlockSpec((tm, d_model), lambda i: (i, 0)),
                  pl.BlockSpec((tm, TOP_K), lambda i: (i, 0)),
                  const(ln_g.shape), const(ln_b.shape),
                  pl.BlockSpec(memory_space=pl.ANY)],
        out_specs=pl.BlockSpec((tm, d_model), lambda i: (i, 0)),
        scratch_shapes=[pltpu.VMEM((2, TOP_K, tm, words), U32),
                        pltpu.SemaphoreType.DMA((2,))],
        compiler_params=_cparams(("arbitrary",)),
        name="combine_ln",
    )(slot3, slot3, slot3, slot3, h1, gate_t, ln_g, ln_b, ys)


def kernel(x, w_in, rpb, s5_a_re, s5_a_im, s5_log_dt, s5_b_re, s5_b_im, s5_c_re, s5_c_im, s5_d, w_glu, b_glu, w_out, ln1_g, ln1_b, w_router_group, b_router_group, w_router_expert, b_router_expert, w_gate, w_up, w_down, ln2_g, ln2_b):
    bsz, seq, d_model = x.shape
    n_tok = bsz * seq
    assert bsz == SUBLANES, "the S5 chunk scan keeps one batch element per sublane"
    assert seq % (ATTN_ROWS_PER_STEP * GRID_W) == 0 and n_tok % TOKEN_TILE == 0
    assert (TOP_K * n_tok) % EXPERT_BLOCK == 0
    assert w_in.shape[0] == DEPTH
    layer = 0
    x2d = x.reshape(n_tok, d_model)
    width = w_in.shape[-1] // 4
    n_heads = width // HEAD_DIM
    n_grp = s5_d.shape[1]

    q, k, v, u = _in_proj(x2d, w_in[layer].astype(BF16))

    shape3 = (bsz, seq, width)
    attn = _attention(q.reshape(shape3), k.reshape(shape3), v.reshape(shape3),
                      _attn_bias_table(rpb[layer]), n_heads).reshape(n_tok, width)

    n_chunks = seq // S5_CHUNK
    u_grp = u.reshape(bsz, n_chunks, S5_CHUNK, n_grp, S5_GROUP_CH)
    u_grp = jnp.transpose(u_grp, (3, 1, 0, 2, 4)).reshape(n_grp, n_chunks * bsz,
                                                          S5_CHUNK * S5_GROUP_CH)
    m_mat, v_mat, lam_re, lam_im = _s5_tables(
        s5_a_re[layer], s5_a_im[layer], s5_log_dt[layer], s5_b_re[layer], s5_b_im[layer],
        s5_c_re[layer], s5_c_im[layer], s5_d[layer])
    ssm = _s5(u_grp, m_mat, v_mat, lam_re, lam_im, n_chunks)
    ssm = ssm.reshape(n_grp, n_chunks, bsz, S5_CHUNK, S5_GROUP_CH)
    ssm = jnp.transpose(ssm, (2, 1, 3, 0, 4)).reshape(n_tok, width)

    n_router = N_EXPERT_GROUPS + N_EXPERTS
    w_router = jnp.concatenate([w_router_group[layer], w_router_expert[layer]], axis=1)
    w_router = jnp.pad(jnp.transpose(w_router), ((0, LANES - n_router), (0, 0))).astype(BF16)
    b_router = jnp.concatenate([b_router_group[layer], b_router_expert[layer]])
    b_router = jnp.pad(b_router, (0, LANES - n_router)).astype(F32).reshape(LANES, 1)
    row = lambda t: t.astype(F32).reshape(1, -1)
    h1, hpack, eid, gate, rank, hist = _mix(
        x2d, attn, ssm, w_glu[layer].astype(BF16), row(b_glu[layer]),
        w_out[layer].astype(BF16), row(ln1_g[layer]), row(ln1_b[layer]), w_router, b_router)

    slot, work, n_work = _dispatch_plan(eid, rank, hist[:, :, 0], n_tok)
    slot3 = slot.reshape(TOP_K * (n_tok // TOKEN_TILE), 1, TOKEN_TILE)
    xs = _dispatch(hpack, slot3)
    ys = _experts(xs, work, n_work, w_gate[layer], w_up[layer], w_down[layer])

    out = _final(h1, ys, slot3, jnp.transpose(gate), row(ln2_g[layer]), row(ln2_b[layer]))
    return out.reshape(bsz, seq, d_model)
```

```python
import functools
import math

import numpy as np
import jax
import jax.numpy as jnp
from jax import lax
from jax.experimental import pallas as pl
from jax.experimental.pallas import tpu as pltpu

F32 = jnp.float32
BF16 = jnp.bfloat16
U32 = jnp.uint32

GRID_W = 64
WIN_H = 8
WIN_W = 16
HEAD_DIM = 64
N_EXPERT_GROUPS = 8
EXPERTS_PER_GROUP = 8
N_EXPERTS = N_EXPERT_GROUPS * EXPERTS_PER_GROUP
TOP_K = 2
S5_GROUP_CH = 16
S5_STATE = 64
LN_EPS = 1e-5
DEPTH = 1
ALPHA = (2.0 * DEPTH) ** 0.25

LANES = 128
SUBLANES = 8
VMEM_LIMIT_BYTES = 56 * 1024 * 1024

TOKEN_TILE = 512
S5_CHUNK = 16
S5_ROW_TILE = 512
ATTN_ROWS_PER_STEP = 8
EXPERT_BLOCK = 256
NEG_BIAS = -1e30


def _cparams(sem):
    return pltpu.CompilerParams(dimension_semantics=sem,
                                vmem_limit_bytes=VMEM_LIMIT_BYTES)


def _inproj_kernel(x_ref, w_ref, q_ref, k_ref, v_ref, u_ref, *, width, scale):
    x = x_ref[...].astype(BF16)
    outs = (q_ref, k_ref, v_ref, u_ref)
    for i, o_ref in enumerate(outs):
        acc = jnp.dot(x, w_ref[:, i * width:(i + 1) * width],
                      preferred_element_type=F32)
        if i == 0:
            acc = acc * scale
        o_ref[...] = acc.astype(BF16)


def _in_proj(x2d, w_in_bf):
    n_tok, d_model = x2d.shape
    width = w_in_bf.shape[1] // 4
    tm = TOKEN_TILE
    out = jax.ShapeDtypeStruct((n_tok, width), BF16)
    o_spec = pl.BlockSpec((tm, width), lambda i: (i, 0))
    return pl.pallas_call(
        functools.partial(_inproj_kernel, width=width, scale=HEAD_DIM ** -0.5),
        out_shape=(out, out, out, out),
        grid=(n_tok // tm,),
        in_specs=[pl.BlockSpec((tm, d_model), lambda i: (i, 0)),
                  pl.BlockSpec((d_model, 4 * width), lambda i: (0, 0))],
        out_specs=(o_spec, o_spec, o_spec, o_spec),
        compiler_params=_cparams(("parallel",)),
        name="in_proj",
    )(x2d, w_in_bf)


def _attn_bias_table(rpb):
    col = np.arange(GRID_W)
    cstart = np.clip(col - WIN_W // 2, 0, GRID_W - WIN_W)
    j = np.arange(GRID_W)
    valid = (j[None, :] >= cstart[:, None]) & (j[None, :] < cstart[:, None] + WIN_W)
    dcol = np.clip(j[None, :] - col[:, None] + (WIN_W - 1), 0, 2 * WIN_W - 2)
    full = rpb.astype(F32)[:, :, dcol]
    full = jnp.where(jnp.asarray(valid)[None, None], full, NEG_BIAS)
    pair = jnp.concatenate([full[:, :-1], full[:, 1:]], axis=-1)
    n_heads = pair.shape[0]
    pair = pair.reshape(n_heads // 2, 2, 2 * WIN_H - 2, GRID_W, LANES)
    return jnp.transpose(pair, (2, 0, 1, 3, 4)).reshape(2 * WIN_H - 2, n_heads // 2,
                                                       2 * GRID_W, LANES)


def _attn_kernel(q_ref, k_ref, v_ref, b_ref, o_ref, *, n_rows, n_heads):
    rblk = pl.program_id(1)
    band = WIN_H * GRID_W
    n_pairs = n_heads // 2
    lane = lax.broadcasted_iota(jnp.int32, (GRID_W, LANES), 1)
    low_half = lane < HEAD_DIM

    def row_body(rr, carry):
        r = rblk * ATTN_ROWS_PER_STEP + rr
        rs = jnp.clip(r - WIN_H // 2, 0, n_rows - WIN_H)
        shift = rs - r + (WIN_H - 1)
        q_off = pl.multiple_of(rr * GRID_W, GRID_W)
        k_off = pl.multiple_of(rs * GRID_W, GRID_W)
        scores = []
        for hp in range(n_pairs):
            cols = slice(hp * LANES, (hp + 1) * LANES)
            qp = q_ref[pl.ds(q_off, GRID_W), cols]
            zero = jnp.zeros_like(qp)
            q2 = jnp.concatenate([jnp.where(low_half, qp, zero), jnp.where(low_half, zero, qp)],
                                 axis=0)
            kp = k_ref[pl.ds(k_off, band), cols]
            scores.append(lax.dot_general(q2, kp, (((1,), (1,)), ((), ())),
                                          preferred_element_type=F32))
        probs, dens = [], []
        for hp in range(n_pairs):
            s = jnp.concatenate(
                [scores[hp][:, m * LANES:(m + 1) * LANES] + b_ref[shift + 2 * m, hp]
                 for m in range(band // LANES)], axis=-1)
            mx = jnp.max(s, axis=-1, keepdims=True)
            p = jnp.exp(s - mx)
            dens.append(jnp.sum(p, axis=-1, keepdims=True))
            probs.append(p.astype(BF16))
        for hp in range(n_pairs):
            cols = slice(hp * LANES, (hp + 1) * LANES)
            vp = v_ref[pl.ds(k_off, band), cols]
            o = jnp.dot(probs[hp], vp, preferred_element_type=F32) / dens[hp]
            o_ref[pl.ds(q_off, GRID_W), cols] = jnp.where(
                low_half, o[:GRID_W], o[GRID_W:]).astype(BF16)
        return carry

    lax.fori_loop(0, ATTN_ROWS_PER_STEP, row_body, 0, unroll=True)


def _attention(q, k, v, bias_tab, n_heads):
    bsz, seq, width = q.shape
    n_rows = seq // GRID_W
    tq = ATTN_ROWS_PER_STEP * GRID_W
    return pl.pallas_call(
        functools.partial(_attn_kernel, n_rows=n_rows, n_heads=n_heads),
        out_shape=jax.ShapeDtypeStruct((bsz, seq, width), BF16),
        grid=(bsz, seq // tq),
        in_specs=[pl.BlockSpec((None, tq, width), lambda b, r: (b, r, 0)),
                  pl.BlockSpec((None, seq, width), lambda b, r: (b, 0, 0)),
                  pl.BlockSpec((None, seq, width), lambda b, r: (b, 0, 0)),
                  pl.BlockSpec(bias_tab.shape, lambda b, r: (0, 0, 0, 0))],
        out_specs=pl.BlockSpec((None, tq, width), lambda b, r: (b, r, 0)),
        compiler_params=_cparams(("parallel", "arbitrary")),
        name="nbr_attention",
    )(q, k, v, bias_tab)


def _s5_tables(a_re, a_im, log_dt, b_re, b_im, c_re, c_im, d_skip):
    hi = lax.Precision.HIGHEST
    L = S5_CHUNK
    n_grp, n_state = a_re.shape[1], a_re.shape[2]
    n_ch = b_re.shape[-1]
    f = lambda t: t.astype(F32)
    ks = jnp.arange(L + 1, dtype=F32)

    t_dirs, w_re, w_im, v_re, v_im, lam_l_re, lam_l_im = [], [], [], [], [], [], []
    for direction in range(2):
        ar, ai = f(a_re[direction]), f(a_im[direction])
        dt = jnp.exp(f(log_dt[direction]))[:, None]
        mag = jnp.exp(ar * dt)
        lr, li = mag * jnp.cos(ai * dt), mag * jnp.sin(ai * dt)
        den = ar * ar + ai * ai
        zr = ((lr - 1.0) * ar + li * ai) / den
        zi = (li * ar - (lr - 1.0) * ai) / den
        br, bi = f(b_re[direction]), f(b_im[direction])
        bbr = zr[..., None] * br - zi[..., None] * bi
        bbi = zr[..., None] * bi + zi[..., None] * br
        cr, ci = f(c_re[direction]), f(c_im[direction])
        pmag = jnp.exp(ks[:, None, None] * (ar * dt)[None])
        pr = pmag * jnp.cos(ks[:, None, None] * (ai * dt)[None])
        pi = pmag * jnp.sin(ks[:, None, None] * (ai * dt)[None])
        lbr = pr[..., None] * bbr[None] - pi[..., None] * bbi[None]
        lbi = pr[..., None] * bbi[None] + pi[..., None] * bbr[None]
        kern = (jnp.einsum('gcp,kgpd->kgcd', cr, lbr, precision=hi)
                - jnp.einsum('gcp,kgpd->kgcd', ci, lbi, precision=hi))
        s_idx = np.arange(L)[:, None]
        t_idx = np.arange(L)[None, :]
        lag = (t_idx - s_idx) if direction == 0 else (s_idx - t_idx)
        tk = kern[np.clip(lag, 0, L)]
        tk = jnp.where(jnp.asarray(lag >= 0)[:, :, None, None, None], tk, 0.0)
        t_dirs.append(jnp.transpose(tk, (2, 0, 4, 1, 3)))
        pw = (L - 1 - np.arange(L)) if direction == 0 else np.arange(L)
        w_re.append(jnp.transpose(lbr[pw], (1, 0, 3, 2)))
        w_im.append(jnp.transpose(lbi[pw], (1, 0, 3, 2)))
        po = (np.arange(L) + 1) if direction == 0 else (L - np.arange(L))
        clr = cr[None] * pr[po][:, :, None, :] - ci[None] * pi[po][:, :, None, :]
        cli = cr[None] * pi[po][:, :, None, :] + ci[None] * pr[po][:, :, None, :]
        v_re.append(jnp.transpose(clr, (1, 3, 0, 2)))
        v_im.append(jnp.transpose(-cli, (1, 3, 0, 2)))
        lam_l_re.append(pr[L])
        lam_l_im.append(pi[L])

    lc = L * n_ch
    eye = jnp.eye(lc, dtype=F32).reshape(L, n_ch, L, n_ch)
    toep = t_dirs[0] + t_dirs[1] + eye[None] * f(d_skip)[:, None, None, None, :]
    toep = toep.reshape(n_grp, lc, lc)
    w_all = jnp.concatenate([w_re[0], w_re[1], w_im[0], w_im[1]], axis=-1)
    m_mat = jnp.concatenate([toep, w_all.reshape(n_grp, lc, 4 * n_state)], axis=-1)
    v_mat = jnp.concatenate([v_re[0], v_re[1], v_im[0], v_im[1]], axis=1)
    v_mat = v_mat.reshape(n_grp, 4 * n_state, lc)
    lam_re = jnp.concatenate([lam_l_re[0], lam_l_re[1]], axis=-1)
    lam_im = jnp.concatenate([lam_l_im[0], lam_l_im[1]], axis=-1)
    bc = lambda t: jnp.broadcast_to(t[:, None, :], (n_grp, SUBLANES, 2 * n_state))
    return m_mat.astype(BF16), v_mat.astype(BF16), bc(lam_re), bc(lam_im)


def _gelu_tanh(x):
    cdf = 0.5 * (1.0 + jnp.tanh(math.sqrt(2.0 / math.pi) * (x + 0.044715 * (x * x * x))))
    return x * cdf


def _s5_kernel(u_ref, m_ref, v_ref, lr_ref, li_ref, o_ref, y_scr, x_scr, h_scr,
               *, n_chunks, n_state):
    n_rows = u_ref.shape[0]
    lc = o_ref.shape[1]
    rt = min(S5_ROW_TILE, n_rows)

    def proj_body(i, carry):
        rows = pl.ds(pl.multiple_of(i * rt, rt), rt)
        xy = jnp.dot(u_ref[rows, :], m_ref[...], preferred_element_type=F32)
        y_scr[rows, :] = xy[:, :lc]
        x_scr[rows, :] = xy[:, lc:]
        return carry

    lax.fori_loop(0, n_rows // rt, proj_body, 0)

    lam_re = lr_ref[...]
    lam_im = li_ref[...]
    two_p = 2 * n_state
    is_fwd = lax.broadcasted_iota(jnp.int32, (SUBLANES, two_p), 1) < n_state

    def scan_body(j, carry):
        h_re, h_im = carry
        rf = pl.ds(pl.multiple_of(j * SUBLANES, SUBLANES), SUBLANES)
        rb = pl.ds(pl.multiple_of((n_chunks - 1 - j) * SUBLANES, SUBLANES), SUBLANES)
        x_re = jnp.where(is_fwd, x_scr[rf, 0:two_p], x_scr[rb, 0:two_p])
        x_im = jnp.where(is_fwd, x_scr[rf, two_p:2 * two_p], x_scr[rb, two_p:2 * two_p])
        h_scr[rf, 0:n_state] = h_re[:, 0:n_state]
        h_scr[rb, n_state:two_p] = h_re[:, n_state:two_p]
        h_scr[rf, two_p:two_p + n_state] = h_im[:, 0:n_state]
        h_scr[rb, two_p + n_state:2 * two_p] = h_im[:, n_state:two_p]
        n_re = lam_re * h_re - lam_im * h_im + x_re
        n_im = lam_re * h_im + lam_im * h_re + x_im
        return n_re, n_im

    zero = jnp.zeros((SUBLANES, two_p), F32)
    lax.fori_loop(0, n_chunks, scan_body, (zero, zero))

    def out_body(i, carry):
        rows = pl.ds(pl.multiple_of(i * rt, rt), rt)
        y = y_scr[rows, :] + jnp.dot(h_scr[rows, :].astype(BF16), v_ref[...],
                                     preferred_element_type=F32)
        o_ref[rows, :] = _gelu_tanh(y).astype(BF16)
        return carry

    lax.fori_loop(0, n_rows // rt, out_body, 0)


def _s5(u_grp, m_mat, v_mat, lam_re, lam_im, n_chunks):
    n_grp, n_rows, lc = u_grp.shape
    four_p = v_mat.shape[1]
    n_state = four_p // 4
    return pl.pallas_call(
        functools.partial(_s5_kernel, n_chunks=n_chunks, n_state=n_state),
        out_shape=jax.ShapeDtypeStruct((n_grp, n_rows, lc), BF16),
        grid=(n_grp,),
        in_specs=[pl.BlockSpec((None, n_rows, lc), lambda g: (g, 0, 0)),
                  pl.BlockSpec((None, lc, lc + four_p), lambda g: (g, 0, 0)),
                  pl.BlockSpec((None, four_p, lc), lambda g: (g, 0, 0)),
                  pl.BlockSpec((None, SUBLANES, 2 * n_state), lambda g: (g, 0, 0)),
                  pl.BlockSpec((None, SUBLANES, 2 * n_state), lambda g: (g, 0, 0))],
        out_specs=pl.BlockSpec((None, n_rows, lc), lambda g: (g, 0, 0)),
        scratch_shapes=[pltpu.VMEM((n_rows, lc), F32),
                        pltpu.VMEM((n_rows, four_p), F32),
                        pltpu.VMEM((n_rows, four_p), F32)],
        compiler_params=_cparams(("parallel",)),
        name="s5_chunked",
    )(u_grp, m_mat, v_mat, lam_re, lam_im)


def _layer_norm(x, g, b):
    mu = jnp.mean(x, axis=-1, keepdims=True)
    xc = x - mu
    var = jnp.mean(xc * xc, axis=-1, keepdims=True)
    return xc * lax.rsqrt(var + LN_EPS) * g + b


def _pack_bf16_pair(lo, hi):
    lo_bits = lax.bitcast_convert_type(lo.astype(BF16).astype(F32), U32) >> 16
    hi_bits = lax.bitcast_convert_type(hi.astype(BF16).astype(F32), U32) & jnp.uint32(0xFFFF0000)
    return lo_bits | hi_bits


def _unpack_bf16_pair(word):
    lo = lax.bitcast_convert_type(word << 16, F32)
    hi = lax.bitcast_convert_type(word & jnp.uint32(0xFFFF0000), F32)
    return lo, hi


def _first_index_of_max(vals, row_id, n):
    mx = jnp.max(vals, axis=0, keepdims=True)
    idx = jnp.min(jnp.where(vals == mx, row_id, n), axis=0, keepdims=True)
    return mx, idx


def _mix_kernel(x_ref, a_ref, s_ref, wglu_ref, bglu_ref, wout_ref, g_ref, b_ref,
                wr_ref, br_ref, tri_ref, h_ref, hp_ref, eid_ref, gate_ref, rank_ref, hist_ref):
    ssm = s_ref[...]
    half = ssm.shape[1]
    z = jnp.dot(ssm, wglu_ref[...], preferred_element_type=F32) + bglu_ref[...]
    glu = ssm.astype(F32) * jax.nn.sigmoid(z)
    mix = (jnp.dot(a_ref[...], wout_ref[0:half, :], preferred_element_type=F32)
           + jnp.dot(glu.astype(BF16), wout_ref[half:, :], preferred_element_type=F32))
    h = _layer_norm(ALPHA * x_ref[...] + mix, g_ref[...], b_ref[...])
    h_ref[...] = h
    d_half = h.shape[1] // 2
    hp_ref[...] = _pack_bf16_pair(h[:, :d_half], h[:, d_half:])

    logits = lax.dot_general(wr_ref[...], h.astype(BF16), (((1,), (1,)), ((), ())),
                             preferred_element_type=F32) + br_ref[...]
    ng, epg = N_EXPERT_GROUPS, EXPERTS_PER_GROUP
    tm = logits.shape[1]
    row_id = lax.broadcasted_iota(jnp.int32, (ng, tm), 0)
    g_logit = logits[0:ng, :]
    g_max, g_idx = _first_index_of_max(g_logit, row_id, ng)
    g_val = 1.0 / jnp.sum(jnp.exp(g_logit - g_max), axis=0, keepdims=True)
    e_in = jnp.zeros((epg, tm), F32)
    for gi in range(ng):
        e_in = jnp.where(g_idx == gi, logits[ng + gi * epg:ng + (gi + 1) * epg, :], e_in)
    m1, i1 = _first_index_of_max(e_in, row_id, epg)
    rest = jnp.where(row_id == i1, -jnp.inf, e_in)
    m2, i2 = _first_index_of_max(rest, row_id, epg)
    e2 = jnp.exp(m2 - m1)
    w1 = 1.0 / (1.0 + e2)
    w2 = e2 / (1.0 + e2)
    eids = (g_idx * epg + i1, g_idx * epg + i2)
    gate_ref[0:1, :] = g_val * w1
    gate_ref[1:2, :] = g_val * w2

    exp_id = lax.broadcasted_iota(jnp.int32, (N_EXPERTS, tm), 0)
    before = jnp.zeros((N_EXPERTS, 1), F32)
    for kk in range(TOP_K):
        onehot = exp_id == eids[kk]
        prefix = jnp.dot(onehot.astype(BF16), tri_ref[...], preferred_element_type=F32)
        rank = jnp.sum(jnp.where(onehot, prefix + before, 0.0), axis=0, keepdims=True)
        eid_ref[kk:kk + 1, :] = eids[kk]
        rank_ref[kk:kk + 1, :] = rank.astype(jnp.int32)
        before = before + jnp.sum(onehot.astype(F32), axis=1, keepdims=True)
    hist_ref[...] = jnp.broadcast_to(before, hist_ref.shape).astype(jnp.int32)


def _mix(x2d, attn, ssm, w_glu, b_glu, w_out, ln_g, ln_b, w_router, b_router):
    n_tok, d_model = x2d.shape
    half = attn.shape[1]
    tm = TOKEN_TILE
    n_tiles = n_tok // tm
    tri = jnp.asarray(np.triu(np.ones((tm, tm), np.float32), k=1), dtype=BF16)
    const = lambda shape: pl.BlockSpec(shape, lambda i: (0,) * len(shape))
    sel = lambda: pl.BlockSpec((TOP_K, tm), lambda i: (0, i))
    return pl.pallas_call(
        _mix_kernel,
        out_shape=(jax.ShapeDtypeStruct((n_tok, d_model), F32),
                   jax.ShapeDtypeStruct((n_tok, d_model // 2), U32),
                   jax.ShapeDtypeStruct((TOP_K, n_tok), jnp.int32),
                   jax.ShapeDtypeStruct((TOP_K, n_tok), F32),
                   jax.ShapeDtypeStruct((TOP_K, n_tok), jnp.int32),
                   jax.ShapeDtypeStruct((n_tiles, N_EXPERTS, LANES), jnp.int32)),
        grid=(n_tiles,),
        in_specs=[pl.BlockSpec((tm, d_model), lambda i: (i, 0)),
                  pl.BlockSpec((tm, half), lambda i: (i, 0)),
                  pl.BlockSpec((tm, half), lambda i: (i, 0)),
                  const(w_glu.shape), const(b_glu.shape), const(w_out.shape),
                  const(ln_g.shape), const(ln_b.shape),
                  const(w_router.shape), const(b_router.shape), const(tri.shape)],
        out_specs=(pl.BlockSpec((tm, d_model), lambda i: (i, 0)),
                   pl.BlockSpec((tm, d_model // 2), lambda i: (i, 0)),
                   sel(), sel(), sel(),
                   pl.BlockSpec((None, N_EXPERTS, LANES), lambda i: (i, 0, 0))),
        compiler_params=_cparams(("parallel",)),
        name="mix_ln_router",
    )(x2d, attn, ssm, w_glu, b_glu, w_out, ln_g, ln_b, w_router, b_router, tri)


def _dispatch_plan(eid, rank, hist, n_tok):
    bm = EXPERT_BLOCK
    n_assign = TOP_K * n_tok
    tm = TOKEN_TILE
    n_tiles = n_tok // tm
    counts = jnp.sum(hist, axis=0)
    ends = jnp.cumsum(counts)
    off = ends - counts
    base = off[None, :] + jnp.cumsum(hist, axis=0) - hist
    eid_t = eid.reshape(TOP_K, n_tiles, tm)
    onehot = eid_t[..., None] == jnp.arange(N_EXPERTS, dtype=jnp.int32)
    slot = rank.reshape(TOP_K, n_tiles, tm) + jnp.sum(
        jnp.where(onehot, base[None, :, None, :], 0), axis=-1)
    slot = slot.astype(jnp.int32)

    n_mtiles = n_assign // bm
    n_work = n_mtiles + N_EXPERTS
    t_lo = off // bm
    n_vis = jnp.where(counts > 0, (ends + bm - 1) // bm - t_lo, 0)
    w_end = jnp.cumsum(n_vis)
    w_start = w_end - n_vis
    n_valid = w_end[-1]
    w = jnp.arange(n_work, dtype=jnp.int32)
    wc = jnp.minimum(w, n_valid - 1)
    e_w = jnp.minimum(jnp.sum((w_end[None, :] <= wc[:, None]).astype(jnp.int32), axis=1),
                      N_EXPERTS - 1)
    pick = e_w[:, None] == jnp.arange(N_EXPERTS, dtype=jnp.int32)[None, :]
    take = lambda t: jnp.sum(jnp.where(pick, t[None, :], 0), axis=1)
    tile_w = (take(t_lo) + (wc - take(w_start))).astype(jnp.int32)
    valid = w < n_valid
    lo_w = jnp.where(valid, take(off), 0).astype(jnp.int32)
    hi_w = jnp.where(valid, take(ends), 0).astype(jnp.int32)
    prev_tile = jnp.concatenate([jnp.full((1,), -1, jnp.int32), tile_w[:-1]])
    first_w = (tile_w != prev_tile).astype(jnp.int32)
    prev_e = jnp.concatenate([jnp.full((1,), -1, jnp.int32), e_w[:-1]])
    newexp_w = (e_w != prev_e).astype(jnp.int32)
    work = dict(tile=tile_w, expert=e_w, lo=lo_w, hi=hi_w, first=first_w, newexp=newexp_w,
                valid=valid.astype(jnp.int32))
    return slot, work, n_work


def _dispatch_kernel(s0_ref, s1_ref, p0_ref, p1_ref, hp_ref, xs_hbm, ring, sem):
    i = pl.program_id(0)
    n_steps = pl.num_programs(0)
    tm = hp_ref.shape[0]
    cur = lax.rem(i, 2)

    def row_copy(r, s_ref, which):
        return pltpu.make_async_copy(ring.at[which, pl.ds(r, 1), :],
                                     xs_hbm.at[pl.ds(s_ref[0, r], 1), :], sem.at[which])

    def wait_tile(refs, which):
        for r in range(tm):
            for s_ref in refs:
                row_copy(r, s_ref, which).wait()

    ring[cur] = hp_ref[...]
    for r in range(tm):
        row_copy(r, s0_ref, cur).start()
        row_copy(r, s1_ref, cur).start()

    @pl.when(i >= 1)
    def _():
        wait_tile((p0_ref, p1_ref), 1 - cur)

    @pl.when(i == n_steps - 1)
    def _():
        wait_tile((s0_ref, s1_ref), cur)


def _dispatch(hpack, slot3):
    n_tok, words = hpack.shape
    tm = TOKEN_TILE
    n_tiles = n_tok // tm
    smem = lambda k, d: pl.BlockSpec(
        (None, 1, tm), lambda i: (jnp.maximum(i - d, 0) + k * n_tiles, 0, 0),
        memory_space=pltpu.SMEM)
    return pl.pallas_call(
        _dispatch_kernel,
        out_shape=jax.ShapeDtypeStruct((TOP_K * n_tok, words), U32),
        grid=(n_tiles,),
        in_specs=[smem(0, 0), smem(1, 0), smem(0, 1), smem(1, 1),
                  pl.BlockSpec((tm, words), lambda i: (i, 0))],
        out_specs=pl.BlockSpec(memory_space=pl.ANY),
        scratch_shapes=[pltpu.VMEM((2, tm, words), U32),
                        pltpu.SemaphoreType.DMA((2,))],
        compiler_params=_cparams(("arbitrary",)),
        name="moe_dispatch",
    )(slot3, slot3, slot3, slot3, hpack)


def _expert_kernel(tile_ref, exp_ref, lo_ref, hi_ref, first_ref, newexp_ref, valid_ref,
                   x_ref, wg_ref, wu_ref, wd_ref, o_ref, wg_bf, wu_bf, wd_bf):
    w = pl.program_id(0)
    bm, words = x_ref.shape

    @pl.when(valid_ref[w] == 1)
    def _():
        @pl.when(newexp_ref[w] == 1)
        def _():
            wg_bf[...] = wg_ref[...].astype(BF16)
            wu_bf[...] = wu_ref[...].astype(BF16)
            wd_bf[...] = wd_ref[...].astype(BF16)

        x_lo, x_hi = _unpack_bf16_pair(x_ref[...])
        x_lo, x_hi = x_lo.astype(BF16), x_hi.astype(BF16)
        gate = (jnp.dot(x_lo, wg_bf[0:words, :], preferred_element_type=F32)
                + jnp.dot(x_hi, wg_bf[words:, :], preferred_element_type=F32))
        up = (jnp.dot(x_lo, wu_bf[0:words, :], preferred_element_type=F32)
              + jnp.dot(x_hi, wu_bf[words:, :], preferred_element_type=F32))
        hid = (jax.nn.silu(gate) * up).astype(BF16)
        y = jnp.dot(hid, wd_bf[...], preferred_element_type=F32)
        packed = _pack_bf16_pair(y[:, :words], y[:, words:])
        row = tile_ref[w] * bm + lax.broadcasted_iota(jnp.int32, (bm, words), 0)
        mine = jnp.logical_and(row >= lo_ref[w], row < hi_ref[w])

        @pl.when(first_ref[w] == 1)
        def _():
            o_ref[...] = jnp.where(mine, packed, jnp.zeros_like(packed))

        @pl.when(first_ref[w] == 0)
        def _():
            o_ref[...] = jnp.where(mine, packed, o_ref[...])


def _experts(xs, work, n_work, w_gate, w_up, w_down):
    n_rows, words = xs.shape
    d_model, d_exp = w_gate.shape[1], w_gate.shape[2]
    bm = EXPERT_BLOCK
    names = ("tile", "expert", "lo", "hi", "first", "newexp", "valid")
    n_pre = len(names)
    row_map = lambda w, *pre: (pre[0][w], 0)
    exp_map = lambda w, *pre: (pre[1][w], 0, 0)
    grid_spec = pltpu.PrefetchScalarGridSpec(
        num_scalar_prefetch=n_pre,
        grid=(n_work,),
        in_specs=[pl.BlockSpec((bm, words), row_map),
                  pl.BlockSpec((None, d_model, d_exp), exp_map),
                  pl.BlockSpec((None, d_model, d_exp), exp_map),
                  pl.BlockSpec((None, d_exp, d_model), exp_map)],
        out_specs=pl.BlockSpec((bm, words), row_map),
        scratch_shapes=[pltpu.VMEM((d_model, d_exp), BF16),
                        pltpu.VMEM((d_model, d_exp), BF16),
                        pltpu.VMEM((d_exp, d_model), BF16)])
    return pl.pallas_call(
        _expert_kernel,
        out_shape=jax.ShapeDtypeStruct((n_rows, words), U32),
        grid_spec=grid_spec,
        compiler_params=_cparams(("arbitrary",)),
        name="expert_ffn",
    )(*[work[n] for n in names], xs, w_gate, w_up, w_down)


def _final_kernel(s0_ref, s1_ref, n0_ref, n1_ref, h_ref, gate_ref, g_ref, b_ref, ys_hbm,
                  o_ref, buf, sem):
    i = pl.program_id(0)
    n_steps = pl.num_programs(0)
    tm = h_ref.shape[0]
    cur = lax.rem(i, 2)

    def row_copy(r, k, s_ref, which):
        return pltpu.make_async_copy(ys_hbm.at[pl.ds(s_ref[0, r], 1), :],
                                     buf.at[which, k, pl.ds(r, 1), :], sem.at[which])

    def start_tile(refs, which):
        for r in range(tm):
            for k in range(TOP_K):
                row_copy(r, k, refs[k], which).start()

    def wait_tile(refs, which):
        for r in range(tm):
            for k in range(TOP_K):
                row_copy(r, k, refs[k], which).wait()

    @pl.when(i == 0)
    def _():
        start_tile((s0_ref, s1_ref), 0)

    start_tile((n0_ref, n1_ref), 1 - cur)
    wait_tile((s0_ref, s1_ref), cur)

    gate = gate_ref[...]
    y0_lo, y0_hi = _unpack_bf16_pair(buf[cur, 0])
    y1_lo, y1_hi = _unpack_bf16_pair(buf[cur, 1])
    g0, g1 = gate[:, 0:1], gate[:, 1:2]
    y = jnp.concatenate([y0_lo * g0 + y1_lo * g1, y0_hi * g0 + y1_hi * g1], axis=-1)
    o_ref[...] = _layer_norm(ALPHA * h_ref[...] + y, g_ref[...], b_ref[...])

    @pl.when(i == n_steps - 1)
    def _():
        wait_tile((n0_ref, n1_ref), 1 - cur)


def _final(h1, ys, slot3, gate_t, ln_g, ln_b):
    n_tok, d_model = h1.shape
    words = ys.shape[1]
    tm = TOKEN_TILE
    n_tiles = n_tok // tm
    const = lambda shape: pl.BlockSpec(shape, lambda i: (0,) * len(shape))
    cur = lambda k: pl.BlockSpec((None, 1, tm), lambda i: (i + k * n_tiles, 0, 0),
                                 memory_space=pltpu.SMEM)
    nxt = lambda k: pl.BlockSpec(
        (None, 1, tm), lambda i: (jnp.minimum(i + 1, n_tiles - 1) + k * n_tiles, 0, 0),
        memory_space=pltpu.SMEM)
    return pl.pallas_call(
        _final_kernel,
        out_shape=jax.ShapeDtypeStruct((n_tok, d_model), F32),
        grid=(n_tiles,),
        in_specs=[cur(0), cur(1), nxt(0), nxt(1),
                  pl.BlockSpec((tm, d_model), lambda i: (i, 0)),
                  pl.BlockSpec((tm, TOP_K), lambda i: (i, 0)),
                  const(ln_g.shape), const(ln_b.shape),
                  pl.BlockSpec(memory_space=pl.ANY)],
        out_specs=pl.BlockSpec((tm, d_model), lambda i: (i, 0)),
        scratch_shapes=[pltpu.VMEM((2, TOP_K, tm, words), U32),
                        pltpu.SemaphoreType.DMA((2,))],
        compiler_params=_cparams(("arbitrary",)),
        name="combine_ln",
    )(slot3, slot3, slot3, slot3, h1, gate_t, ln_g, ln_b, ys)


def kernel(x, w_in, rpb, s5_a_re, s5_a_im, s5_log_dt, s5_b_re, s5_b_im, s5_c_re, s5_c_im, s5_d, w_glu, b_glu, w_out, ln1_g, ln1_b, w_router_group, b_router_group, w_router_expert, b_router_expert, w_gate, w_up, w_down, ln2_g, ln2_b):
    bsz, seq, d_model = x.shape
    n_tok = bsz * seq
    assert bsz == SUBLANES, "the S5 chunk scan keeps one batch element per sublane"
    assert seq % (ATTN_ROWS_PER_STEP * GRID_W) == 0 and n_tok % TOKEN_TILE == 0
    assert (TOP_K * n_tok) % EXPERT_BLOCK == 0
    assert w_in.shape[0] == DEPTH
    layer = 0
    x2d = x.reshape(n_tok, d_model)
    width = w_in.shape[-1] // 4
    n_heads = width // HEAD_DIM
    n_grp = s5_d.shape[1]

    q, k, v, u = _in_proj(x2d, w_in[layer].astype(BF16))

    shape3 = (bsz, seq, width)
    attn = _attention(q.reshape(shape3), k.reshape(shape3), v.reshape(shape3),
                      _attn_bias_table(rpb[layer]), n_heads).reshape(n_tok, width)

    n_chunks = seq // S5_CHUNK
    u_grp = u.reshape(bsz, n_chunks, S5_CHUNK, n_grp, S5_GROUP_CH)
    u_grp = jnp.transpose(u_grp, (3, 1, 0, 2, 4)).reshape(n_grp, n_chunks * bsz,
                                                          S5_CHUNK * S5_GROUP_CH)
    m_mat, v_mat, lam_re, lam_im = _s5_tables(
        s5_a_re[layer], s5_a_im[layer], s5_log_dt[layer], s5_b_re[layer], s5_b_im[layer],
        s5_c_re[layer], s5_c_im[layer], s5_d[layer])
    ssm = _s5(u_grp, m_mat, v_mat, lam_re, lam_im, n_chunks)
    ssm = ssm.reshape(n_grp, n_chunks, bsz, S5_CHUNK, S5_GROUP_CH)
    ssm = jnp.transpose(ssm, (2, 1, 3, 0, 4)).reshape(n_tok, width)

    n_router = N_EXPERT_GROUPS + N_EXPERTS
    w_router = jnp.concatenate([w_router_group[layer], w_router_expert[layer]], axis=1)
    w_router = jnp.pad(jnp.transpose(w_router), ((0, LANES - n_router), (0, 0))).astype(BF16)
    b_router = jnp.concatenate([b_router_group[layer], b_router_expert[layer]])
    b_router = jnp.pad(b_router, (0, LANES - n_router)).astype(F32).reshape(LANES, 1)
    row = lambda t: t.astype(F32).reshape(1, -1)
    h1, hpack, eid, gate, rank, hist = _mix(
        x2d, attn, ssm, w_glu[layer].astype(BF16), row(b_glu[layer]),
        w_out[layer].astype(BF16), row(ln1_g[layer]), row(ln1_b[layer]), w_router, b_router)

    slot, work, n_work = _dispatch_plan(eid, rank, hist[:, :, 0], n_tok)
    slot3 = slot.reshape(TOP_K * (n_tok // TOKEN_TILE), 1, TOKEN_TILE)
    xs = _dispatch(hpack, slot3)
    ys = _experts(xs, work, n_work, w_gate[layer], w_up[layer], w_down[layer])

    out = _final(h1, ys, slot3, jnp.transpose(gate), row(ln2_g[layer]), row(ln2_b[layer]))
    return out.reshape(bsz, seq, d_model)
```

```python
import functools
import math

import numpy as np
import jax
import jax.numpy as jnp
from jax import lax
from jax.experimental import pallas as pl
from jax.experimental.pallas import tpu as pltpu

F32 = jnp.float32
BF16 = jnp.bfloat16
U32 = jnp.uint32

GRID_W = 64
WIN_H = 8
WIN_W = 16
HEAD_DIM = 64
N_EXPERT_GROUPS = 8
EXPERTS_PER_GROUP = 8
N_EXPERTS = N_EXPERT_GROUPS * EXPERTS_PER_GROUP
TOP_K = 2
S5_GROUP_CH = 16
S5_STATE = 64
LN_EPS = 1e-5
DEPTH = 1
ALPHA = (2.0 * DEPTH) ** 0.25

LANES = 128
SUBLANES = 8
VMEM_LIMIT_BYTES = 56 * 1024 * 1024

TOKEN_TILE = 512
S5_CHUNK = 16
S5_TOKEN_TILE = S5_CHUNK * LANES
S5_SWAP_ROWS = 64
ATTN_ROWS_PER_STEP = 8
EXPERT_BLOCK = 256
NEG_BIAS = -1e30


def _cparams(sem):
    return pltpu.CompilerParams(dimension_semantics=sem,
                                vmem_limit_bytes=VMEM_LIMIT_BYTES)


def _inproj_kernel(x_ref, w_ref, q_ref, k_ref, v_ref, *, width, scale):
    x = x_ref[...].astype(BF16)
    outs = (q_ref, k_ref, v_ref)
    for i, o_ref in enumerate(outs):
        acc = jnp.dot(x, w_ref[:, i * width:(i + 1) * width],
                      preferred_element_type=F32)
        if i == 0:
            acc = acc * scale
        o_ref[...] = acc.astype(BF16)


def _in_proj(x2d, w_qkv_bf):
    n_tok, d_model = x2d.shape
    width = w_qkv_bf.shape[1] // 3
    tm = TOKEN_TILE
    out = jax.ShapeDtypeStruct((n_tok, width), BF16)
    o_spec = pl.BlockSpec((tm, width), lambda i: (i, 0))
    return pl.pallas_call(
        functools.partial(_inproj_kernel, width=width, scale=HEAD_DIM ** -0.5),
        out_shape=(out, out, out),
        grid=(n_tok // tm,),
        in_specs=[pl.BlockSpec((tm, d_model), lambda i: (i, 0)),
                  pl.BlockSpec((d_model, 3 * width), lambda i: (0, 0))],
        out_specs=(o_spec, o_spec, o_spec),
        compiler_params=_cparams(("parallel",)),
        name="in_proj",
    )(x2d, w_qkv_bf)


def _uproj_kernel(x_ref, w_ref, o_ref):
    tm, d_model = x_ref.shape
    nc = tm // S5_CHUNK
    xp = pltpu.einshape("jtd->tjd", x_ref[...].reshape(nc, S5_CHUNK, d_model))
    xp = xp.reshape(tm, d_model).astype(BF16)
    ut = lax.dot_general(w_ref[...], xp, (((1,), (1,)), ((), ())),
                         preferred_element_type=F32)
    n_grp = o_ref.shape[0]
    for s in range(S5_CHUNK):
        o_ref[:, s, :, :] = ut[:, s * nc:(s + 1) * nc].astype(BF16).reshape(n_grp, S5_GROUP_CH, nc)


def _u_proj(x2d, w_u_t_bf, n_grp):
    n_tok, d_model = x2d.shape
    tm = S5_TOKEN_TILE
    nc = tm // S5_CHUNK
    return pl.pallas_call(
        _uproj_kernel,
        out_shape=jax.ShapeDtypeStruct((n_grp, S5_CHUNK, S5_GROUP_CH, n_tok // S5_CHUNK), BF16),
        grid=(n_tok // tm,),
        in_specs=[pl.BlockSpec((tm, d_model), lambda i: (i, 0)),
                  pl.BlockSpec(w_u_t_bf.shape, lambda i: (0, 0))],
        out_specs=pl.BlockSpec((n_grp, S5_CHUNK, S5_GROUP_CH, nc), lambda i: (0, 0, 0, i)),
        compiler_params=_cparams(("parallel",)),
        name="u_proj_t",
    )(x2d, w_u_t_bf)


def _attn_bias_table(rpb):
    col = np.arange(GRID_W)
    cstart = np.clip(col - WIN_W // 2, 0, GRID_W - WIN_W)
    j = np.arange(GRID_W)
    valid = (j[None, :] >= cstart[:, None]) & (j[None, :] < cstart[:, None] + WIN_W)
    dcol = np.clip(j[None, :] - col[:, None] + (WIN_W - 1), 0, 2 * WIN_W - 2)
    full = rpb.astype(F32)[:, :, dcol]
    full = jnp.where(jnp.asarray(valid)[None, None], full, NEG_BIAS)
    pair = jnp.concatenate([full[:, :-1], full[:, 1:]], axis=-1)
    n_heads = pair.shape[0]
    pair = pair.reshape(n_heads // 2, 2, 2 * WIN_H - 2, GRID_W, LANES)
    return jnp.transpose(pair, (2, 0, 1, 3, 4)).reshape(2 * WIN_H - 2, n_heads // 2,
                                                       2 * GRID_W, LANES)


def _attn_kernel(q_ref, k_ref, v_ref, b_ref, o_ref, *, n_rows, n_heads):
    rblk = pl.program_id(1)
    band = WIN_H * GRID_W
    n_pairs = n_heads // 2
    lane = lax.broadcasted_iota(jnp.int32, (GRID_W, LANES), 1)
    low_half = lane < HEAD_DIM

    def row_body(rr, carry):
        r = rblk * ATTN_ROWS_PER_STEP + rr
        rs = jnp.clip(r - WIN_H // 2, 0, n_rows - WIN_H)
        shift = rs - r + (WIN_H - 1)
        q_off = pl.multiple_of(rr * GRID_W, GRID_W)
        k_off = pl.multiple_of(rs * GRID_W, GRID_W)
        scores = []
        for hp in range(n_pairs):
            cols = slice(hp * LANES, (hp + 1) * LANES)
            qp = q_ref[pl.ds(q_off, GRID_W), cols]
            zero = jnp.zeros_like(qp)
            q2 = jnp.concatenate([jnp.where(low_half, qp, zero), jnp.where(low_half, zero, qp)],
                                 axis=0)
            kp = k_ref[pl.ds(k_off, band), cols]
            scores.append(lax.dot_general(q2, kp, (((1,), (1,)), ((), ())),
                                          preferred_element_type=F32))
        probs, dens = [], []
        for hp in range(n_pairs):
            s = jnp.concatenate(
                [scores[hp][:, m * LANES:(m + 1) * LANES] + b_ref[shift + 2 * m, hp]
                 for m in range(band // LANES)], axis=-1)
            mx = jnp.max(s, axis=-1, keepdims=True)
            p = jnp.exp(s - mx)
            dens.append(jnp.sum(p, axis=-1, keepdims=True))
            probs.append(p.astype(BF16))
        for hp in range(n_pairs):
            cols = slice(hp * LANES, (hp + 1) * LANES)
            vp = v_ref[pl.ds(k_off, band), cols]
            o = jnp.dot(probs[hp], vp, preferred_element_type=F32) / dens[hp]
            o_ref[pl.ds(q_off, GRID_W), cols] = jnp.where(
                low_half, o[:GRID_W], o[GRID_W:]).astype(BF16)
        return carry

    lax.fori_loop(0, ATTN_ROWS_PER_STEP, row_body, 0, unroll=True)


def _attention(q, k, v, bias_tab, n_heads):
    bsz, seq, width = q.shape
    n_rows = seq // GRID_W
    tq = ATTN_ROWS_PER_STEP * GRID_W
    return pl.pallas_call(
        functools.partial(_attn_kernel, n_rows=n_rows, n_heads=n_heads),
        out_shape=jax.ShapeDtypeStruct((bsz, seq, width), BF16),
        grid=(bsz, seq // tq),
        in_specs=[pl.BlockSpec((None, tq, width), lambda b, r: (b, r, 0)),
                  pl.BlockSpec((None, seq, width), lambda b, r: (b, 0, 0)),
                  pl.BlockSpec((None, seq, width), lambda b, r: (b, 0, 0)),
                  pl.BlockSpec(bias_tab.shape, lambda b, r: (0, 0, 0, 0))],
        out_specs=pl.BlockSpec((None, tq, width), lambda b, r: (b, r, 0)),
        compiler_params=_cparams(("parallel", "arbitrary")),
        name="nbr_attention",
    )(q, k, v, bias_tab)


def _s5_tables(a_re, a_im, log_dt, b_re, b_im, c_re, c_im, d_skip):
    hi = lax.Precision.HIGHEST
    L = S5_CHUNK
    n_grp, n_state = a_re.shape[1], a_re.shape[2]
    n_ch = b_re.shape[-1]
    f = lambda t: t.astype(F32)
    ks = jnp.arange(L + 1, dtype=F32)

    t_dirs, w_re, w_im, v_re, v_im, lam_l_re, lam_l_im = [], [], [], [], [], [], []
    for direction in range(2):
        ar, ai = f(a_re[direction]), f(a_im[direction])
        dt = jnp.exp(f(log_dt[direction]))[:, None]
        mag = jnp.exp(ar * dt)
        lr, li = mag * jnp.cos(ai * dt), mag * jnp.sin(ai * dt)
        den = ar * ar + ai * ai
        zr = ((lr - 1.0) * ar + li * ai) / den
        zi = (li * ar - (lr - 1.0) * ai) / den
        br, bi = f(b_re[direction]), f(b_im[direction])
        bbr = zr[..., None] * br - zi[..., None] * bi
        bbi = zr[..., None] * bi + zi[..., None] * br
        cr, ci = f(c_re[direction]), f(c_im[direction])
        pmag = jnp.exp(ks[:, None, None] * (ar * dt)[None])
        pr = pmag * jnp.cos(ks[:, None, None] * (ai * dt)[None])
        pi = pmag * jnp.sin(ks[:, None, None] * (ai * dt)[None])
        lbr = pr[..., None] * bbr[None] - pi[..., None] * bbi[None]
        lbi = pr[..., None] * bbi[None] + pi[..., None] * bbr[None]
        kern = (jnp.einsum('gcp,kgpd->kgcd', cr, lbr, precision=hi)
                - jnp.einsum('gcp,kgpd->kgcd', ci, lbi, precision=hi))
        s_idx = np.arange(L)[:, None]
        t_idx = np.arange(L)[None, :]
        lag = (t_idx - s_idx) if direction == 0 else (s_idx - t_idx)
        tk = kern[np.clip(lag, 0, L)]
        tk = jnp.where(jnp.asarray(lag >= 0)[:, :, None, None, None], tk, 0.0)
        t_dirs.append(jnp.transpose(tk, (2, 0, 4, 1, 3)))
        pw = (L - 1 - np.arange(L)) if direction == 0 else np.arange(L)
        w_re.append(jnp.transpose(lbr[pw], (1, 0, 3, 2)))
        w_im.append(jnp.transpose(lbi[pw], (1, 0, 3, 2)))
        po = (np.arange(L) + 1) if direction == 0 else (L - np.arange(L))
        clr = cr[None] * pr[po][:, :, None, :] - ci[None] * pi[po][:, :, None, :]
        cli = cr[None] * pi[po][:, :, None, :] + ci[None] * pr[po][:, :, None, :]
        v_re.append(jnp.transpose(clr, (1, 3, 0, 2)))
        v_im.append(jnp.transpose(-cli, (1, 3, 0, 2)))
        lam_l_re.append(pr[L])
        lam_l_im.append(pi[L])

    lc = L * n_ch
    eye = jnp.eye(lc, dtype=F32).reshape(L, n_ch, L, n_ch)
    toep = t_dirs[0] + t_dirs[1] + eye[None] * f(d_skip)[:, None, None, None, :]
    toep = toep.reshape(n_grp, lc, lc)
    w_all = jnp.concatenate([w_re[0], w_re[1], w_im[0], w_im[1]], axis=-1)
    w_mat = w_all.reshape(n_grp, lc, 4 * n_state)
    v_mat = jnp.concatenate([v_re[0], v_re[1], v_im[0], v_im[1]], axis=1)
    v_mat = v_mat.reshape(n_grp, 4 * n_state, lc)
    lam_re = jnp.concatenate([lam_l_re[0], lam_l_re[1]], axis=-1)
    lam_im = jnp.concatenate([lam_l_im[0], lam_l_im[1]], axis=-1)
    bc = lambda t: jnp.broadcast_to(t[:, None, :], (n_grp, SUBLANES, 2 * n_state))
    return (w_mat.astype(BF16), jnp.swapaxes(toep, 1, 2).astype(BF16),
            jnp.swapaxes(v_mat, 1, 2).astype(BF16), bc(lam_re), bc(lam_im))


def _gelu_tanh(x):
    cdf = 0.5 * (1.0 + jnp.tanh(math.sqrt(2.0 / math.pi) * (x + 0.044715 * (x * x * x))))
    return x * cdf


def _s5_kernel(ut_ref, w_ref, tt_ref, vt_ref, lr_ref, li_ref, o_ref, xn_scr, x_scr, h_scr, hn_scr,
               *, n_chunks, n_state):
    n_batch = xn_scr.shape[0]
    sw = S5_SWAP_ROWS
    two_p = 2 * n_state

    for b in range(n_batch):
        cols = slice(b * n_chunks, (b + 1) * n_chunks)
        xn_scr[b] = lax.dot_general(ut_ref[:, cols], w_ref[...], (((0,), (0,)), ((), ())),
                                    preferred_element_type=F32)
    for jb in range(n_chunks // sw):
        rows = slice(jb * sw, (jb + 1) * sw)
        x_scr[rows] = pltpu.einshape("bjl->jbl", xn_scr[:, rows, :])

    lam_re = lr_ref[...]
    lam_im = li_ref[...]
    is_fwd = lax.broadcasted_iota(jnp.int32, (n_batch, two_p), 1) < n_state

    def scan_body(j, carry):
        h_re, h_im = carry
        jr = n_chunks - 1 - j
        x_re = jnp.where(is_fwd, x_scr[j, :, 0:two_p], x_scr[jr, :, 0:two_p])
        x_im = jnp.where(is_fwd, x_scr[j, :, two_p:2 * two_p], x_scr[jr, :, two_p:2 * two_p])
        h_scr[j, :, 0:n_state] = h_re[:, 0:n_state]
        h_scr[jr, :, n_state:two_p] = h_re[:, n_state:two_p]
        h_scr[j, :, two_p:two_p + n_state] = h_im[:, 0:n_state]
        h_scr[jr, :, two_p + n_state:2 * two_p] = h_im[:, n_state:two_p]
        n_re = lam_re * h_re - lam_im * h_im + x_re
        n_im = lam_re * h_im + lam_im * h_re + x_im
        return n_re, n_im

    zero = jnp.zeros((n_batch, two_p), F32)
    lax.fori_loop(0, n_chunks, scan_body, (zero, zero))

    for jb in range(n_chunks // sw):
        rows = slice(jb * sw, (jb + 1) * sw)
        hn_scr[:, rows, :] = pltpu.einshape("jbl->bjl", h_scr[rows])
    for b in range(n_batch):
        cols = slice(b * n_chunks, (b + 1) * n_chunks)
        y = (jnp.dot(tt_ref[...], ut_ref[:, cols], preferred_element_type=F32)
             + lax.dot_general(vt_ref[...], hn_scr[b].astype(BF16), (((1,), (1,)), ((), ())),
                               preferred_element_type=F32))
        o_ref[:, cols] = _gelu_tanh(y).astype(BF16)


def _s5(ut, w_mat, t_t, v_t, lam_re, lam_im, n_batch, n_chunks):
    n_grp, lc, n_cols = ut.shape
    four_p = w_mat.shape[2]
    n_state = four_p // 4
    grp = lambda shape: pl.BlockSpec((None,) + shape, lambda g: (g, 0, 0))
    return pl.pallas_call(
        functools.partial(_s5_kernel, n_chunks=n_chunks, n_state=n_state),
        out_shape=jax.ShapeDtypeStruct((n_grp, lc, n_cols), BF16),
        grid=(n_grp,),
        in_specs=[grp((lc, n_cols)), grp((lc, four_p)), grp((lc, lc)), grp((lc, four_p)),
                  grp((SUBLANES, 2 * n_state)), grp((SUBLANES, 2 * n_state))],
        out_specs=grp((lc, n_cols)),
        scratch_shapes=[pltpu.VMEM((n_batch, n_chunks, four_p), F32),
                        pltpu.VMEM((n_chunks, n_batch, four_p), F32),
                        pltpu.VMEM((n_chunks, n_batch, four_p), F32),
                        pltpu.VMEM((n_batch, n_chunks, four_p), F32)],
        compiler_params=_cparams(("parallel",)),
        name="s5_chunked",
    )(ut, w_mat, t_t, v_t, lam_re, lam_im)


def _glu_kernel(yt_ref, wt_ref, b_ref, o_ref):
    n_grp, lc, nc = yt_ref.shape
    n_ch = lc // S5_CHUNK
    width = n_grp * n_ch
    ssm_t = jnp.concatenate(
        [yt_ref[:, t * n_ch:(t + 1) * n_ch, :].reshape(width, nc) for t in range(S5_CHUNK)], axis=1)
    z = jnp.dot(wt_ref[...], ssm_t, preferred_element_type=F32) + b_ref[...]
    glu_t = ssm_t.astype(F32) * jax.nn.sigmoid(z)
    glu = jnp.transpose(glu_t).reshape(S5_CHUNK, nc, width)
    o_ref[...] = pltpu.einshape("tjd->jtd", glu).reshape(S5_CHUNK * nc, width).astype(BF16)


def _glu(yt, w_glu_t_bf, b_glu_col):
    n_grp, lc, n_cols = yt.shape
    nc = LANES
    width = w_glu_t_bf.shape[0]
    return pl.pallas_call(
        _glu_kernel,
        out_shape=jax.ShapeDtypeStruct((n_cols * S5_CHUNK, width), BF16),
        grid=(n_cols // nc,),
        in_specs=[pl.BlockSpec((n_grp, lc, nc), lambda i: (0, 0, i)),
                  pl.BlockSpec(w_glu_t_bf.shape, lambda i: (0, 0)),
                  pl.BlockSpec(b_glu_col.shape, lambda i: (0, 0))],
        out_specs=pl.BlockSpec((nc * S5_CHUNK, width), lambda i: (i, 0)),
        compiler_params=_cparams(("parallel",)),
        name="glu_to_tokens",
    )(yt, w_glu_t_bf, b_glu_col)


def _layer_norm(x, g, b):
    mu = jnp.mean(x, axis=-1, keepdims=True)
    xc = x - mu
    var = jnp.mean(xc * xc, axis=-1, keepdims=True)
    return xc * lax.rsqrt(var + LN_EPS) * g + b


def _pack_bf16_pair(lo, hi):
    lo_bits = lax.bitcast_convert_type(lo.astype(BF16).astype(F32), U32) >> 16
    hi_bits = lax.bitcast_convert_type(hi.astype(BF16).astype(F32), U32) & jnp.uint32(0xFFFF0000)
    return lo_bits | hi_bits


def _unpack_bf16_pair(word):
    lo = lax.bitcast_convert_type(word << 16, F32)
    hi = lax.bitcast_convert_type(word & jnp.uint32(0xFFFF0000), F32)
    return lo, hi


def _first_index_of_max(vals, row_id, n):
    mx = jnp.max(vals, axis=0, keepdims=True)
    idx = jnp.min(jnp.where(vals == mx, row_id, n), axis=0, keepdims=True)
    return mx, idx


def _mix_kernel(x_ref, a_ref, s_ref, wout_ref, g_ref, b_ref,
                wr_ref, br_ref, tri_ref, h_ref, hp_ref, eid_ref, gate_ref, rank_ref, hist_ref):
    half = a_ref.shape[1]
    mix = (jnp.dot(a_ref[...], wout_ref[0:half, :], preferred_element_type=F32)
           + jnp.dot(s_ref[...], wout_ref[half:, :], preferred_element_type=F32))
    h = _layer_norm(ALPHA * x_ref[...] + mix, g_ref[...], b_ref[...])
    h_ref[...] = h
    d_half = h.shape[1] // 2
    hp_ref[...] = _pack_bf16_pair(h[:, :d_half], h[:, d_half:])

    logits = lax.dot_general(wr_ref[...], h.astype(BF16), (((1,), (1,)), ((), ())),
                             preferred_element_type=F32) + br_ref[...]
    ng, epg = N_EXPERT_GROUPS, EXPERTS_PER_GROUP
    tm = logits.shape[1]
    row_id = lax.broadcasted_iota(jnp.int32, (ng, tm), 0)
    g_logit = logits[0:ng, :]
    g_max, g_idx = _first_index_of_max(g_logit, row_id, ng)
    g_val = 1.0 / jnp.sum(jnp.exp(g_logit - g_max), axis=0, keepdims=True)
    e_in = jnp.zeros((epg, tm), F32)
    for gi in range(ng):
        e_in = jnp.where(g_idx == gi, logits[ng + gi * epg:ng + (gi + 1) * epg, :], e_in)
    m1, i1 = _first_index_of_max(e_in, row_id, epg)
    rest = jnp.where(row_id == i1, -jnp.inf, e_in)
    m2, i2 = _first_index_of_max(rest, row_id, epg)
    e2 = jnp.exp(m2 - m1)
    w1 = 1.0 / (1.0 + e2)
    w2 = e2 / (1.0 + e2)
    eids = (g_idx * epg + i1, g_idx * epg + i2)
    gate_ref[0:1, :] = g_val * w1
    gate_ref[1:2, :] = g_val * w2

    exp_id = lax.broadcasted_iota(jnp.int32, (N_EXPERTS, tm), 0)
    before = jnp.zeros((N_EXPERTS, 1), F32)
    for kk in range(TOP_K):
        onehot = exp_id == eids[kk]
        prefix = jnp.dot(onehot.astype(BF16), tri_ref[...], preferred_element_type=F32)
        rank = jnp.sum(jnp.where(onehot, prefix + before, 0.0), axis=0, keepdims=True)
        eid_ref[kk:kk + 1, :] = eids[kk]
        rank_ref[kk:kk + 1, :] = rank.astype(jnp.int32)
        before = before + jnp.sum(onehot.astype(F32), axis=1, keepdims=True)
    hist_ref[...] = jnp.broadcast_to(before, hist_ref.shape).astype(jnp.int32)


def _mix(x2d, attn, glu, w_out, ln_g, ln_b, w_router, b_router):
    n_tok, d_model = x2d.shape
    half = attn.shape[1]
    tm = TOKEN_TILE
    n_tiles = n_tok // tm
    tri = jnp.asarray(np.triu(np.ones((tm, tm), np.float32), k=1), dtype=BF16)
    const = lambda shape: pl.BlockSpec(shape, lambda i: (0,) * len(shape))
    sel = lambda: pl.BlockSpec((TOP_K, tm), lambda i: (0, i))
    return pl.pallas_call(
        _mix_kernel,
        out_shape=(jax.ShapeDtypeStruct((n_tok, d_model), F32),
                   jax.ShapeDtypeStruct((n_tok, d_model // 2), U32),
                   jax.ShapeDtypeStruct((TOP_K, n_tok), jnp.int32),
                   jax.ShapeDtypeStruct((TOP_K, n_tok), F32),
                   jax.ShapeDtypeStruct((TOP_K, n_tok), jnp.int32),
                   jax.ShapeDtypeStruct((n_tiles, N_EXPERTS, LANES), jnp.int32)),
        grid=(n_tiles,),
        in_specs=[pl.BlockSpec((tm, d_model), lambda i: (i, 0)),
                  pl.BlockSpec((tm, half), lambda i: (i, 0)),
                  pl.BlockSpec((tm, half), lambda i: (i, 0)),
                  const(w_out.shape), const(ln_g.shape), const(ln_b.shape),
                  const(w_router.shape), const(b_router.shape), const(tri.shape)],
        out_specs=(pl.BlockSpec((tm, d_model), lambda i: (i, 0)),
                   pl.BlockSpec((tm, d_model // 2), lambda i: (i, 0)),
                   sel(), sel(), sel(),
                   pl.BlockSpec((None, N_EXPERTS, LANES), lambda i: (i, 0, 0))),
        compiler_params=_cparams(("parallel",)),
        name="mix_ln_router",
    )(x2d, attn, glu, w_out, ln_g, ln_b, w_router, b_router, tri)


def _dispatch_plan(eid, rank, hist, n_tok):
    bm = EXPERT_BLOCK
    n_assign = TOP_K * n_tok
    tm = TOKEN_TILE
    n_tiles = n_tok // tm
    counts = jnp.sum(hist, axis=0)
    ends = jnp.cumsum(counts)
    off = ends - counts
    base = off[None, :] + jnp.cumsum(hist, axis=0) - hist
    eid_t = eid.reshape(TOP_K, n_tiles, tm)
    onehot = eid_t[..., None] == jnp.arange(N_EXPERTS, dtype=jnp.int32)
    slot = rank.reshape(TOP_K, n_tiles, tm) + jnp.sum(
        jnp.where(onehot, base[None, :, None, :], 0), axis=-1)
    slot = slot.astype(jnp.int32)

    n_mtiles = n_assign // bm
    n_work = n_mtiles + N_EXPERTS
    t_lo = off // bm
    n_vis = jnp.where(counts > 0, (ends + bm - 1) // bm - t_lo, 0)
    w_end = jnp.cumsum(n_vis)
    w_start = w_end - n_vis
    n_valid = w_end[-1]
    w = jnp.arange(n_work, dtype=jnp.int32)
    wc = jnp.minimum(w, n_valid - 1)
    e_w = jnp.minimum(jnp.sum((w_end[None, :] <= wc[:, None]).astype(jnp.int32), axis=1),
                      N_EXPERTS - 1)
    pick = e_w[:, None] == jnp.arange(N_EXPERTS, dtype=jnp.int32)[None, :]
    take = lambda t: jnp.sum(jnp.where(pick, t[None, :], 0), axis=1)
    tile_w = (take(t_lo) + (wc - take(w_start))).astype(jnp.int32)
    valid = w < n_valid
    lo_w = jnp.where(valid, take(off), 0).astype(jnp.int32)
    hi_w = jnp.where(valid, take(ends), 0).astype(jnp.int32)
    prev_tile = jnp.concatenate([jnp.full((1,), -1, jnp.int32), tile_w[:-1]])
    first_w = (tile_w != prev_tile).astype(jnp.int32)
    prev_e = jnp.concatenate([jnp.full((1,), -1, jnp.int32), e_w[:-1]])
    newexp_w = (e_w != prev_e).astype(jnp.int32)
    work = dict(tile=tile_w, expert=e_w, lo=lo_w, hi=hi_w, first=first_w, newexp=newexp_w,
                valid=valid.astype(jnp.int32))
    return slot, work, n_work


def _dispatch_kernel(s0_ref, s1_ref, p0_ref, p1_ref, hp_ref, xs_hbm, ring, sem):
    i = pl.program_id(0)
    n_steps = pl.num_programs(0)
    tm = hp_ref.shape[0]
    cur = lax.rem(i, 2)

    def row_copy(r, s_ref, which):
        return pltpu.make_async_copy(ring.at[which, pl.ds(r, 1), :],
                                     xs_hbm.at[pl.ds(s_ref[0, r], 1), :], sem.at[which])

    def wait_tile(refs, which):
        for r in range(tm):
            for s_ref in refs:
                row_copy(r, s_ref, which).wait()

    ring[cur] = hp_ref[...]
    for r in range(tm):
        row_copy(r, s0_ref, cur).start()
        row_copy(r, s1_ref, cur).start()

    @pl.when(i >= 1)
    def _():
        wait_tile((p0_ref, p1_ref), 1 - cur)

    @pl.when(i == n_steps - 1)
    def _():
        wait_tile((s0_ref, s1_ref), cur)


def _dispatch(hpack, slot3):
    n_tok, words = hpack.shape
    tm = TOKEN_TILE
    n_tiles = n_tok // tm
    smem = lambda k, d: pl.BlockSpec(
        (None, 1, tm), lambda i: (jnp.maximum(i - d, 0) + k * n_tiles, 0, 0),
        memory_space=pltpu.SMEM)
    return pl.pallas_call(
        _dispatch_kernel,
        out_shape=jax.ShapeDtypeStruct((TOP_K * n_tok, words), U32),
        grid=(n_tiles,),
        in_specs=[smem(0, 0), smem(1, 0), smem(0, 1), smem(1, 1),
                  pl.BlockSpec((tm, words), lambda i: (i, 0))],
        out_specs=pl.BlockSpec(memory_space=pl.ANY),
        scratch_shapes=[pltpu.VMEM((2, tm, words), U32),
                        pltpu.SemaphoreType.DMA((2,))],
        compiler_params=_cparams(("arbitrary",)),
        name="moe_dispatch",
    )(slot3, slot3, slot3, slot3, hpack)


def _expert_kernel(tile_ref, exp_ref, lo_ref, hi_ref, first_ref, newexp_ref, valid_ref,
                   x_ref, wg_ref, wu_ref, wd_ref, o_ref, wg_bf, wu_bf, wd_bf):
    w = pl.program_id(0)
    bm, words = x_ref.shape

    @pl.when(valid_ref[w] == 1)
    def _():
        @pl.when(newexp_ref[w] == 1)
        def _():
            wg_bf[...] = wg_ref[...].astype(BF16)
            wu_bf[...] = wu_ref[...].astype(BF16)
            wd_bf[...] = wd_ref[...].astype(BF16)

        x_lo, x_hi = _unpack_bf16_pair(x_ref[...])
        x_lo, x_hi = x_lo.astype(BF16), x_hi.astype(BF16)
        gate = (jnp.dot(x_lo, wg_bf[0:words, :], preferred_element_type=F32)
                + jnp.dot(x_hi, wg_bf[words:, :], preferred_element_type=F32))
        up = (jnp.dot(x_lo, wu_bf[0:words, :], preferred_element_type=F32)
              + jnp.dot(x_hi, wu_bf[words:, :], preferred_element_type=F32))
        hid = (jax.nn.silu(gate) * up).astype(BF16)
        y = jnp.dot(hid, wd_bf[...], preferred_element_type=F32)
        packed = _pack_bf16_pair(y[:, :words], y[:, words:])
        row = tile_ref[w] * bm + lax.broadcasted_iota(jnp.int32, (bm, words), 0)
        mine = jnp.logical_and(row >= lo_ref[w], row < hi_ref[w])

        @pl.when(first_ref[w] == 1)
        def _():
            o_ref[...] = jnp.where(mine, packed, jnp.zeros_like(packed))

        @pl.when(first_ref[w] == 0)
        def _():
            o_ref[...] = jnp.where(mine, packed, o_ref[...])


def _experts(xs, work, n_work, w_gate, w_up, w_down):
    n_rows, words = xs.shape
    d_model, d_exp = w_gate.shape[1], w_gate.shape[2]
    bm = EXPERT_BLOCK
    names = ("tile", "expert", "lo", "hi", "first", "newexp", "valid")
    n_pre = len(names)
    row_map = lambda w, *pre: (pre[0][w], 0)
    exp_map = lambda w, *pre: (pre[1][w], 0, 0)
    grid_spec = pltpu.PrefetchScalarGridSpec(
        num_scalar_prefetch=n_pre,
        grid=(n_work,),
        in_specs=[pl.BlockSpec((bm, words), row_map),
                  pl.BlockSpec((None, d_model, d_exp), exp_map),
                  pl.BlockSpec((None, d_model, d_exp), exp_map),
                  pl.BlockSpec((None, d_exp, d_model), exp_map)],
        out_specs=pl.BlockSpec((bm, words), row_map),
        scratch_shapes=[pltpu.VMEM((d_model, d_exp), BF16),
                        pltpu.VMEM((d_model, d_exp), BF16),
                        pltpu.VMEM((d_exp, d_model), BF16)])
    return pl.pallas_call(
        _expert_kernel,
        out_shape=jax.ShapeDtypeStruct((n_rows, words), U32),
        grid_spec=grid_spec,
        compiler_params=_cparams(("arbitrary",)),
        name="expert_ffn",
    )(*[work[n] for n in names], xs, w_gate, w_up, w_down)


def _final_kernel(s0_ref, s1_ref, n0_ref, n1_ref, h_ref, gate_ref, g_ref, b_ref, ys_hbm,
                  o_ref, buf, sem):
    i = pl.program_id(0)
    n_steps = pl.num_programs(0)
    tm = h_ref.shape[0]
    cur = lax.rem(i, 2)

    def row_copy(r, k, s_ref, which):
        return pltpu.make_async_copy(ys_hbm.at[pl.ds(s_ref[0, r], 1), :],
                                     buf.at[which, k, pl.ds(r, 1), :], sem.at[which])

    def start_tile(refs, which):
        for r in range(tm):
            for k in range(TOP_K):
                row_copy(r, k, refs[k], which).start()

    def wait_tile(refs, which):
        for r in range(tm):
            for k in range(TOP_K):
                row_copy(r, k, refs[k], which).wait()

    @pl.when(i == 0)
    def _():
        start_tile((s0_ref, s1_ref), 0)

    start_tile((n0_ref, n1_ref), 1 - cur)
    wait_tile((s0_ref, s1_ref), cur)

    gate = gate_ref[...]
    y0_lo, y0_hi = _unpack_bf16_pair(buf[cur, 0])
    y1_lo, y1_hi = _unpack_bf16_pair(buf[cur, 1])
    g0, g1 = gate[:, 0:1], gate[:, 1:2]
    y = jnp.concatenate([y0_lo * g0 + y1_lo * g1, y0_hi * g0 + y1_hi * g1], axis=-1)
    o_ref[...] = _layer_norm(ALPHA * h_ref[...] + y, g_ref[...], b_ref[...])

    @pl.when(i == n_steps - 1)
    def _():
        wait_tile((n0_ref, n1_ref), 1 - cur)


def _final(h1, ys, slot3, gate_t, ln_g, ln_b):
    n_tok, d_model = h1.shape
    words = ys.shape[1]
    tm = TOKEN_TILE
    n_tiles = n_tok // tm
    const = lambda shape: pl.BlockSpec(shape, lambda i: (0,) * len(shape))
    cur = lambda k: pl.BlockSpec((None, 1, tm), lambda i: (i + k * n_tiles, 0, 0),
                                 memory_space=pltpu.SMEM)
    nxt = lambda k: pl.BlockSpec(
        (None, 1, tm), lambda i: (jnp.minimum(i + 1, n_tiles - 1) + k * n_tiles, 0, 0),
        memory_space=pltpu.SMEM)
    return pl.pallas_call(
        _final_kernel,
        out_shape=jax.ShapeDtypeStruct((n_tok, d_model), F32),
        grid=(n_tiles,),
        in_specs=[cur(0), cur(1), nxt(0), nxt(1),
                  pl.BlockSpec((tm, d_model), lambda i: (i, 0)),
                  pl.BlockSpec((tm, TOP_K), lambda i: (i, 0)),
                  const(ln_g.shape), const(ln_b.shape),
                  pl.BlockSpec(memory_space=pl.ANY)],
        out_specs=pl.BlockSpec((tm, d_model), lambda i: (i, 0)),
        scratch_shapes=[pltpu.VMEM((2, TOP_K, tm, words), U32),
                        pltpu.SemaphoreType.DMA((2,))],
        compiler_params=_cparams(("arbitrary",)),
        name="combine_ln",
    )(slot3, slot3, slot3, slot3, h1, gate_t, ln_g, ln_b, ys)


def kernel(x, w_in, rpb, s5_a_re, s5_a_im, s5_log_dt, s5_b_re, s5_b_im, s5_c_re, s5_c_im, s5_d, w_glu, b_glu, w_out, ln1_g, ln1_b, w_router_group, b_router_group, w_router_expert, b_router_expert, w_gate, w_up, w_down, ln2_g, ln2_b):
    bsz, seq, d_model = x.shape
    n_tok = bsz * seq
    assert bsz == SUBLANES, "the S5 chunk scan keeps one batch element per sublane"
    assert seq % (ATTN_ROWS_PER_STEP * GRID_W) == 0 and n_tok % TOKEN_TILE == 0
    assert (TOP_K * n_tok) % EXPERT_BLOCK == 0
    assert (seq // S5_CHUNK) % LANES == 0, "each batch element owns whole lane tiles of chunks"
    assert w_in.shape[0] == DEPTH
    layer = 0
    x2d = x.reshape(n_tok, d_model)
    width = w_in.shape[-1] // 4
    n_heads = width // HEAD_DIM
    n_grp = s5_d.shape[1]

    w_in_bf = w_in[layer].astype(BF16)
    q, k, v = _in_proj(x2d, w_in_bf[:, :3 * width])
    ut = _u_proj(x2d, jnp.transpose(w_in_bf[:, 3 * width:]), n_grp)

    shape3 = (bsz, seq, width)
    attn = _attention(q.reshape(shape3), k.reshape(shape3), v.reshape(shape3),
                      _attn_bias_table(rpb[layer]), n_heads).reshape(n_tok, width)

    n_chunks = seq // S5_CHUNK
    lc = S5_CHUNK * S5_GROUP_CH
    w_mat, t_t, v_t, lam_re, lam_im = _s5_tables(
        s5_a_re[layer], s5_a_im[layer], s5_log_dt[layer], s5_b_re[layer], s5_b_im[layer],
        s5_c_re[layer], s5_c_im[layer], s5_d[layer])
    yt = _s5(ut.reshape(n_grp, lc, bsz * n_chunks), w_mat, t_t, v_t, lam_re, lam_im,
             bsz, n_chunks)
    glu = _glu(yt, jnp.transpose(w_glu[layer]).astype(BF16),
               b_glu[layer].astype(F32).reshape(-1, 1))

    n_router = N_EXPERT_GROUPS + N_EXPERTS
    w_router = jnp.concatenate([w_router_group[layer], w_router_expert[layer]], axis=1)
    w_router = jnp.pad(jnp.transpose(w_router), ((0, LANES - n_router), (0, 0))).astype(BF16)
    b_router = jnp.concatenate([b_router_group[layer], b_router_expert[layer]])
    b_router = jnp.pad(b_router, (0, LANES - n_router)).astype(F32).reshape(LANES, 1)
    row = lambda t: t.astype(F32).reshape(1, -1)
    h1, hpack, eid, gate, rank, hist = _mix(
        x2d, attn, glu, w_out[layer].astype(BF16), row(ln1_g[layer]), row(ln1_b[layer]),
        w_router, b_router)

    slot, work, n_work = _dispatch_plan(eid, rank, hist[:, :, 0], n_tok)
    slot3 = slot.reshape(TOP_K * (n_tok // TOKEN_TILE), 1, TOKEN_TILE)
    xs = _dispatch(hpack, slot3)
    ys = _experts(xs, work, n_work, w_gate[layer], w_up[layer], w_down[layer])

    out = _final(h1, ys, slot3, jnp.transpose(gate), row(ln2_g[layer]), row(ln2_b[layer]))
    return out.reshape(bsz, seq, d_model)
```

```python
import functools
import math

import numpy as np
import jax
import jax.numpy as jnp
from jax import lax
from jax.experimental import pallas as pl
from jax.experimental.pallas import tpu as pltpu

F32 = jnp.float32
BF16 = jnp.bfloat16
U32 = jnp.uint32

GRID_W = 64
WIN_H = 8
WIN_W = 16
HEAD_DIM = 64
N_EXPERT_GROUPS = 8
EXPERTS_PER_GROUP = 8
N_EXPERTS = N_EXPERT_GROUPS * EXPERTS_PER_GROUP
TOP_K = 2
S5_GROUP_CH = 16
S5_STATE = 64
LN_EPS = 1e-5
DEPTH = 1
ALPHA = (2.0 * DEPTH) ** 0.25

LANES = 128
SUBLANES = 8
VMEM_LIMIT_BYTES = 56 * 1024 * 1024

TOKEN_TILE = 512
S5_CHUNK = 16
S5_TOKEN_TILE = S5_CHUNK * LANES
S5_SWAP_ROWS = 64
ATTN_ROWS_PER_STEP = 8
EXPERT_BLOCK = 512
COMBINE_CHUNKS = 16
NEG_BIAS = -1e30


def _cparams(sem):
    return pltpu.CompilerParams(dimension_semantics=sem,
                                vmem_limit_bytes=VMEM_LIMIT_BYTES)


def _inproj_kernel(x_ref, w_ref, q_ref, k_ref, v_ref, *, width, scale):
    x = x_ref[...].astype(BF16)
    outs = (q_ref, k_ref, v_ref)
    for i, o_ref in enumerate(outs):
        acc = jnp.dot(x, w_ref[:, i * width:(i + 1) * width],
                      preferred_element_type=F32)
        if i == 0:
            acc = acc * scale
        o_ref[...] = acc.astype(BF16)


def _in_proj(x2d, w_qkv_bf):
    n_tok, d_model = x2d.shape
    width = w_qkv_bf.shape[1] // 3
    tm = TOKEN_TILE
    out = jax.ShapeDtypeStruct((n_tok, width), BF16)
    o_spec = pl.BlockSpec((tm, width), lambda i: (i, 0))
    return pl.pallas_call(
        functools.partial(_inproj_kernel, width=width, scale=HEAD_DIM ** -0.5),
        out_shape=(out, out, out),
        grid=(n_tok // tm,),
        in_specs=[pl.BlockSpec((tm, d_model), lambda i: (i, 0)),
                  pl.BlockSpec((d_model, 3 * width), lambda i: (0, 0))],
        out_specs=(o_spec, o_spec, o_spec),
        compiler_params=_cparams(("parallel",)),
        name="in_proj",
    )(x2d, w_qkv_bf)


def _uproj_kernel(x_ref, w_ref, o_ref):
    tm, d_model = x_ref.shape
    nc = tm // S5_CHUNK
    xp = pltpu.einshape("jtd->tjd", x_ref[...].reshape(nc, S5_CHUNK, d_model))
    xp = xp.reshape(tm, d_model).astype(BF16)
    ut = lax.dot_general(w_ref[...], xp, (((1,), (1,)), ((), ())),
                         preferred_element_type=F32)
    n_grp = o_ref.shape[0]
    for s in range(S5_CHUNK):
        o_ref[:, s, :, :] = ut[:, s * nc:(s + 1) * nc].astype(BF16).reshape(n_grp, S5_GROUP_CH, nc)


def _u_proj(x2d, w_u_t_bf, n_grp):
    n_tok, d_model = x2d.shape
    tm = S5_TOKEN_TILE
    nc = tm // S5_CHUNK
    return pl.pallas_call(
        _uproj_kernel,
        out_shape=jax.ShapeDtypeStruct((n_grp, S5_CHUNK, S5_GROUP_CH, n_tok // S5_CHUNK), BF16),
        grid=(n_tok // tm,),
        in_specs=[pl.BlockSpec((tm, d_model), lambda i: (i, 0)),
                  pl.BlockSpec(w_u_t_bf.shape, lambda i: (0, 0))],
        out_specs=pl.BlockSpec((n_grp, S5_CHUNK, S5_GROUP_CH, nc), lambda i: (0, 0, 0, i)),
        compiler_params=_cparams(("parallel",)),
        name="u_proj_t",
    )(x2d, w_u_t_bf)


def _attn_bias_table(rpb):
    col = np.arange(GRID_W)
    cstart = np.clip(col - WIN_W // 2, 0, GRID_W - WIN_W)
    j = np.arange(GRID_W)
    valid = (j[None, :] >= cstart[:, None]) & (j[None, :] < cstart[:, None] + WIN_W)
    dcol = j[None, :] - col[:, None] + (WIN_W - 1)
    pick = (dcol[..., None] == np.arange(2 * WIN_W - 1)) & valid[..., None]
    full = jnp.einsum('hdm,cjm->hdcj', rpb.astype(F32), jnp.asarray(pick, F32),
                      precision=lax.Precision.HIGHEST)
    full = jnp.where(jnp.asarray(valid)[None, None], full, NEG_BIAS)
    pair = jnp.concatenate([full[:, :-1], full[:, 1:]], axis=-1)
    n_heads = pair.shape[0]
    pair = pair.reshape(n_heads // 2, 2, 2 * WIN_H - 2, GRID_W, LANES)
    return jnp.transpose(pair, (2, 0, 1, 3, 4)).reshape(2 * WIN_H - 2, n_heads // 2,
                                                       2 * GRID_W, LANES)


def _attn_kernel(q_ref, k_ref, v_ref, b_ref, o_ref, *, n_rows, n_heads):
    rblk = pl.program_id(1)
    band = WIN_H * GRID_W
    n_pairs = n_heads // 2
    lane = lax.broadcasted_iota(jnp.int32, (GRID_W, LANES), 1)
    low_half = lane < HEAD_DIM

    def row_body(rr, carry):
        r = rblk * ATTN_ROWS_PER_STEP + rr
        rs = jnp.clip(r - WIN_H // 2, 0, n_rows - WIN_H)
        shift = rs - r + (WIN_H - 1)
        q_off = pl.multiple_of(rr * GRID_W, GRID_W)
        k_off = pl.multiple_of(rs * GRID_W, GRID_W)
        scores = []
        for hp in range(n_pairs):
            cols = slice(hp * LANES, (hp + 1) * LANES)
            qp = q_ref[pl.ds(q_off, GRID_W), cols]
            zero = jnp.zeros_like(qp)
            q2 = jnp.concatenate([jnp.where(low_half, qp, zero), jnp.where(low_half, zero, qp)],
                                 axis=0)
            kp = k_ref[pl.ds(k_off, band), cols]
            scores.append(lax.dot_general(q2, kp, (((1,), (1,)), ((), ())),
                                          preferred_element_type=F32))
        probs, dens = [], []
        for hp in range(n_pairs):
            s = jnp.concatenate(
                [scores[hp][:, m * LANES:(m + 1) * LANES] + b_ref[shift + 2 * m, hp]
                 for m in range(band // LANES)], axis=-1)
            mx = jnp.max(s, axis=-1, keepdims=True)
            p = jnp.exp(s - mx)
            dens.append(jnp.sum(p, axis=-1, keepdims=True))
            probs.append(p.astype(BF16))
        for hp in range(n_pairs):
            cols = slice(hp * LANES, (hp + 1) * LANES)
            vp = v_ref[pl.ds(k_off, band), cols]
            o = jnp.dot(probs[hp], vp, preferred_element_type=F32) / dens[hp]
            o_ref[pl.ds(q_off, GRID_W), cols] = jnp.where(
                low_half, o[:GRID_W], o[GRID_W:]).astype(BF16)
        return carry

    lax.fori_loop(0, ATTN_ROWS_PER_STEP, row_body, 0, unroll=True)


def _attention(q, k, v, bias_tab, n_heads):
    bsz, seq, width = q.shape
    n_rows = seq // GRID_W
    tq = ATTN_ROWS_PER_STEP * GRID_W
    return pl.pallas_call(
        functools.partial(_attn_kernel, n_rows=n_rows, n_heads=n_heads),
        out_shape=jax.ShapeDtypeStruct((bsz, seq, width), BF16),
        grid=(bsz, seq // tq),
        in_specs=[pl.BlockSpec((None, tq, width), lambda b, r: (b, r, 0)),
                  pl.BlockSpec((None, seq, width), lambda b, r: (b, 0, 0)),
                  pl.BlockSpec((None, seq, width), lambda b, r: (b, 0, 0)),
                  pl.BlockSpec(bias_tab.shape, lambda b, r: (0, 0, 0, 0))],
        out_specs=pl.BlockSpec((None, tq, width), lambda b, r: (b, r, 0)),
        compiler_params=_cparams(("parallel", "arbitrary")),
        name="nbr_attention",
    )(q, k, v, bias_tab)


def _s5_tables(a_re, a_im, log_dt, b_re, b_im, c_re, c_im, d_skip):
    hi = lax.Precision.HIGHEST
    L = S5_CHUNK
    n_grp, n_state = a_re.shape[1], a_re.shape[2]
    n_ch = b_re.shape[-1]
    f = lambda t: t.astype(F32)
    ks = jnp.arange(L + 1, dtype=F32)

    t_dirs, w_re, w_im, v_re, v_im, lam_l_re, lam_l_im = [], [], [], [], [], [], []
    for direction in range(2):
        ar, ai = f(a_re[direction]), f(a_im[direction])
        dt = jnp.exp(f(log_dt[direction]))[:, None]
        mag = jnp.exp(ar * dt)
        lr, li = mag * jnp.cos(ai * dt), mag * jnp.sin(ai * dt)
        den = ar * ar + ai * ai
        zr = ((lr - 1.0) * ar + li * ai) / den
        zi = (li * ar - (lr - 1.0) * ai) / den
        br, bi = f(b_re[direction]), f(b_im[direction])
        bbr = zr[..., None] * br - zi[..., None] * bi
        bbi = zr[..., None] * bi + zi[..., None] * br
        cr, ci = f(c_re[direction]), f(c_im[direction])
        pmag = jnp.exp(ks[:, None, None] * (ar * dt)[None])
        pr = pmag * jnp.cos(ks[:, None, None] * (ai * dt)[None])
        pi = pmag * jnp.sin(ks[:, None, None] * (ai * dt)[None])
        lbr = pr[..., None] * bbr[None] - pi[..., None] * bbi[None]
        lbi = pr[..., None] * bbi[None] + pi[..., None] * bbr[None]
        c_cat = jnp.concatenate([cr, -ci], axis=-1)
        lb_cat = jnp.concatenate([lbr, lbi], axis=2)
        lb_cat = jnp.transpose(lb_cat, (1, 2, 0, 3)).reshape(n_grp, 2 * n_state, (L + 1) * n_ch)
        kern = jnp.einsum('gcq,gqn->gcn', c_cat, lb_cat, precision=hi)
        kern = jnp.transpose(kern.reshape(n_grp, n_ch, L + 1, n_ch), (2, 0, 1, 3))
        s_idx = np.arange(L)[:, None]
        t_idx = np.arange(L)[None, :]
        lag = (t_idx - s_idx) if direction == 0 else (s_idx - t_idx)
        lag_pick = jnp.asarray(lag[..., None] == np.arange(L + 1), F32)
        tk = jnp.einsum('stk,kgcd->stgcd', lag_pick, kern, precision=hi)
        t_dirs.append(jnp.transpose(tk, (2, 0, 4, 1, 3)))
        lb_w = (lambda t: jnp.flip(t[:L], 0)) if direction == 0 else (lambda t: t[:L])
        w_re.append(jnp.transpose(lb_w(lbr), (1, 0, 3, 2)))
        w_im.append(jnp.transpose(lb_w(lbi), (1, 0, 3, 2)))
        p_o = (lambda t: t[1:]) if direction == 0 else (lambda t: jnp.flip(t[1:], 0))
        pro, pio = p_o(pr), p_o(pi)
        clr = cr[None] * pro[:, :, None, :] - ci[None] * pio[:, :, None, :]
        cli = cr[None] * pio[:, :, None, :] + ci[None] * pro[:, :, None, :]
        v_re.append(jnp.transpose(clr, (1, 3, 0, 2)))
        v_im.append(jnp.transpose(-cli, (1, 3, 0, 2)))
        lam_l_re.append(pr[L])
        lam_l_im.append(pi[L])

    lc = L * n_ch
    eye = jnp.eye(lc, dtype=F32).reshape(L, n_ch, L, n_ch)
    toep = t_dirs[0] + t_dirs[1] + eye[None] * f(d_skip)[:, None, None, None, :]
    toep = toep.reshape(n_grp, lc, lc)
    w_all = jnp.concatenate([w_re[0], w_re[1], w_im[0], w_im[1]], axis=-1)
    w_mat = w_all.reshape(n_grp, lc, 4 * n_state)
    v_mat = jnp.concatenate([v_re[0], v_re[1], v_im[0], v_im[1]], axis=1)
    v_mat = v_mat.reshape(n_grp, 4 * n_state, lc)
    lam_re = jnp.concatenate([lam_l_re[0], lam_l_re[1]], axis=-1)
    lam_im = jnp.concatenate([lam_l_im[0], lam_l_im[1]], axis=-1)
    bc = lambda t: jnp.broadcast_to(t[:, None, :], (n_grp, SUBLANES, 2 * n_state))
    return (w_mat.astype(BF16), jnp.swapaxes(toep, 1, 2).astype(BF16),
            jnp.swapaxes(v_mat, 1, 2).astype(BF16), bc(lam_re), bc(lam_im))


def _gelu_tanh(x):
    cdf = 0.5 * (1.0 + jnp.tanh(math.sqrt(2.0 / math.pi) * (x + 0.044715 * (x * x * x))))
    return x * cdf


def _s5_kernel(ut_ref, w_ref, tt_ref, vt_ref, lr_ref, li_ref, o_ref, xn_scr, x_scr, h_scr, hn_scr,
               *, n_chunks, n_state):
    n_batch = xn_scr.shape[0]
    sw = S5_SWAP_ROWS
    two_p = 2 * n_state

    for b in range(n_batch):
        cols = slice(b * n_chunks, (b + 1) * n_chunks)
        xn_scr[b] = lax.dot_general(ut_ref[:, cols], w_ref[...], (((0,), (0,)), ((), ())),
                                    preferred_element_type=F32)
    for jb in range(n_chunks // sw):
        rows = slice(jb * sw, (jb + 1) * sw)
        x_scr[rows] = pltpu.einshape("bjl->jbl", xn_scr[:, rows, :])

    lam_re = lr_ref[...]
    lam_im = li_ref[...]
    is_fwd = lax.broadcasted_iota(jnp.int32, (n_batch, two_p), 1) < n_state

    def scan_body(j, carry):
        h_re, h_im = carry
        jr = n_chunks - 1 - j
        x_re = jnp.where(is_fwd, x_scr[j, :, 0:two_p], x_scr[jr, :, 0:two_p])
        x_im = jnp.where(is_fwd, x_scr[j, :, two_p:2 * two_p], x_scr[jr, :, two_p:2 * two_p])
        h_scr[j, :, 0:n_state] = h_re[:, 0:n_state]
        h_scr[jr, :, n_state:two_p] = h_re[:, n_state:two_p]
        h_scr[j, :, two_p:two_p + n_state] = h_im[:, 0:n_state]
        h_scr[jr, :, two_p + n_state:2 * two_p] = h_im[:, n_state:two_p]
        n_re = lam_re * h_re - lam_im * h_im + x_re
        n_im = lam_re * h_im + lam_im * h_re + x_im
        return n_re, n_im

    zero = jnp.zeros((n_batch, two_p), F32)
    lax.fori_loop(0, n_chunks, scan_body, (zero, zero))

    for jb in range(n_chunks // sw):
        rows = slice(jb * sw, (jb + 1) * sw)
        hn_scr[:, rows, :] = pltpu.einshape("jbl->bjl", h_scr[rows])
    for b in range(n_batch):
        cols = slice(b * n_chunks, (b + 1) * n_chunks)
        y = (jnp.dot(tt_ref[...], ut_ref[:, cols], preferred_element_type=F32)
             + lax.dot_general(vt_ref[...], hn_scr[b].astype(BF16), (((1,), (1,)), ((), ())),
                               preferred_element_type=F32))
        o_ref[:, cols] = _gelu_tanh(y).astype(BF16)


def _s5(ut, w_mat, t_t, v_t, lam_re, lam_im, n_batch, n_chunks):
    n_grp, lc, n_cols = ut.shape
    four_p = w_mat.shape[2]
    n_state = four_p // 4
    grp = lambda shape: pl.BlockSpec((None,) + shape, lambda g: (g, 0, 0))
    return pl.pallas_call(
        functools.partial(_s5_kernel, n_chunks=n_chunks, n_state=n_state),
        out_shape=jax.ShapeDtypeStruct((n_grp, lc, n_cols), BF16),
        grid=(n_grp,),
        in_specs=[grp((lc, n_cols)), grp((lc, four_p)), grp((lc, lc)), grp((lc, four_p)),
                  grp((SUBLANES, 2 * n_state)), grp((SUBLANES, 2 * n_state))],
        out_specs=grp((lc, n_cols)),
        scratch_shapes=[pltpu.VMEM((n_batch, n_chunks, four_p), F32),
                        pltpu.VMEM((n_chunks, n_batch, four_p), F32),
                        pltpu.VMEM((n_chunks, n_batch, four_p), F32),
                        pltpu.VMEM((n_batch, n_chunks, four_p), F32)],
        compiler_params=_cparams(("parallel",)),
        name="s5_chunked",
    )(ut, w_mat, t_t, v_t, lam_re, lam_im)


def _glu_kernel(yt_ref, wt_ref, b_ref, o_ref):
    n_grp, lc, nc = yt_ref.shape
    n_ch = lc // S5_CHUNK
    width = n_grp * n_ch
    ssm_t = jnp.concatenate(
        [yt_ref[:, t * n_ch:(t + 1) * n_ch, :].reshape(width, nc) for t in range(S5_CHUNK)], axis=1)
    z = jnp.dot(wt_ref[...], ssm_t, preferred_element_type=F32) + b_ref[...]
    glu_t = ssm_t.astype(F32) * jax.nn.sigmoid(z)
    glu = jnp.transpose(glu_t).reshape(S5_CHUNK, nc, width)
    o_ref[...] = pltpu.einshape("tjd->jtd", glu).reshape(S5_CHUNK * nc, width).astype(BF16)


def _glu(yt, w_glu_t_bf, b_glu_col):
    n_grp, lc, n_cols = yt.shape
    nc = LANES
    width = w_glu_t_bf.shape[0]
    return pl.pallas_call(
        _glu_kernel,
        out_shape=jax.ShapeDtypeStruct((n_cols * S5_CHUNK, width), BF16),
        grid=(n_cols // nc,),
        in_specs=[pl.BlockSpec((n_grp, lc, nc), lambda i: (0, 0, i)),
                  pl.BlockSpec(w_glu_t_bf.shape, lambda i: (0, 0)),
                  pl.BlockSpec(b_glu_col.shape, lambda i: (0, 0))],
        out_specs=pl.BlockSpec((nc * S5_CHUNK, width), lambda i: (i, 0)),
        compiler_params=_cparams(("parallel",)),
        name="glu_to_tokens",
    )(yt, w_glu_t_bf, b_glu_col)


def _layer_norm(x, g, b):
    mu = jnp.mean(x, axis=-1, keepdims=True)
    xc = x - mu
    var = jnp.mean(xc * xc, axis=-1, keepdims=True)
    return xc * lax.rsqrt(var + LN_EPS) * g + b


def _pack_bf16_pair(lo, hi):
    lo_bits = lax.bitcast_convert_type(lo.astype(BF16).astype(F32), U32) >> 16
    hi_bits = lax.bitcast_convert_type(hi.astype(BF16).astype(F32), U32) & jnp.uint32(0xFFFF0000)
    return lo_bits | hi_bits


def _unpack_bf16_pair(word):
    lo = lax.bitcast_convert_type(word << 16, F32)
    hi = lax.bitcast_convert_type(word & jnp.uint32(0xFFFF0000), F32)
    return lo, hi


def _first_index_of_max(vals, row_id, n):
    mx = jnp.max(vals, axis=0, keepdims=True)
    idx = jnp.min(jnp.where(vals == mx, row_id, n), axis=0, keepdims=True)
    return mx, idx


def _mix_kernel(x_ref, a_ref, s_ref, wout_ref, g_ref, b_ref,
                wr_ref, br_ref, tri_ref, h_ref, hp_ref, eid_ref, gate_ref, rank_ref, hist_ref):
    half = a_ref.shape[1]
    mix = (jnp.dot(a_ref[...], wout_ref[0:half, :], preferred_element_type=F32)
           + jnp.dot(s_ref[...], wout_ref[half:, :], preferred_element_type=F32))
    h = _layer_norm(ALPHA * x_ref[...] + mix, g_ref[...], b_ref[...])
    h_ref[...] = h
    d_half = h.shape[1] // 2
    hp_ref[...] = _pack_bf16_pair(h[:, :d_half], h[:, d_half:])

    logits = lax.dot_general(wr_ref[...], h.astype(BF16), (((1,), (1,)), ((), ())),
                             preferred_element_type=F32) + br_ref[...]
    ng, epg = N_EXPERT_GROUPS, EXPERTS_PER_GROUP
    tm = logits.shape[1]
    row_id = lax.broadcasted_iota(jnp.int32, (ng, tm), 0)
    g_logit = logits[0:ng, :]
    g_max, g_idx = _first_index_of_max(g_logit, row_id, ng)
    g_val = 1.0 / jnp.sum(jnp.exp(g_logit - g_max), axis=0, keepdims=True)
    e_in = jnp.zeros((epg, tm), F32)
    for gi in range(ng):
        e_in = jnp.where(g_idx == gi, logits[ng + gi * epg:ng + (gi + 1) * epg, :], e_in)
    m1, i1 = _first_index_of_max(e_in, row_id, epg)
    rest = jnp.where(row_id == i1, -jnp.inf, e_in)
    m2, i2 = _first_index_of_max(rest, row_id, epg)
    e2 = jnp.exp(m2 - m1)
    w1 = 1.0 / (1.0 + e2)
    w2 = e2 / (1.0 + e2)
    eids = (g_idx * epg + i1, g_idx * epg + i2)
    gate_ref[0:1, :] = g_val * w1
    gate_ref[1:2, :] = g_val * w2

    exp_id = lax.broadcasted_iota(jnp.int32, (N_EXPERTS, tm), 0)
    before = jnp.zeros((N_EXPERTS, 1), F32)
    for kk in range(TOP_K):
        onehot = exp_id == eids[kk]
        prefix = jnp.dot(onehot.astype(BF16), tri_ref[...], preferred_element_type=F32)
        rank = jnp.sum(jnp.where(onehot, prefix + before, 0.0), axis=0, keepdims=True)
        eid_ref[kk:kk + 1, :] = eids[kk]
        rank_ref[kk:kk + 1, :] = rank.astype(jnp.int32)
        before = before + jnp.sum(onehot.astype(F32), axis=1, keepdims=True)
    hist_ref[...] = jnp.broadcast_to(before, hist_ref.shape).astype(jnp.int32)


def _mix(x2d, attn, glu, w_out, ln_g, ln_b, w_router, b_router):
    n_tok, d_model = x2d.shape
    half = attn.shape[1]
    tm = TOKEN_TILE
    n_tiles = n_tok // tm
    tri = jnp.asarray(np.triu(np.ones((tm, tm), np.float32), k=1), dtype=BF16)
    const = lambda shape: pl.BlockSpec(shape, lambda i: (0,) * len(shape))
    sel = lambda: pl.BlockSpec((TOP_K, tm), lambda i: (0, i))
    return pl.pallas_call(
        _mix_kernel,
        out_shape=(jax.ShapeDtypeStruct((n_tok, d_model), F32),
                   jax.ShapeDtypeStruct((n_tok, d_model // 2), U32),
                   jax.ShapeDtypeStruct((TOP_K, n_tok), jnp.int32),
                   jax.ShapeDtypeStruct((TOP_K, n_tok), F32),
                   jax.ShapeDtypeStruct((TOP_K, n_tok), jnp.int32),
                   jax.ShapeDtypeStruct((n_tiles, N_EXPERTS, LANES), jnp.int32)),
        grid=(n_tiles,),
        in_specs=[pl.BlockSpec((tm, d_model), lambda i: (i, 0)),
                  pl.BlockSpec((tm, half), lambda i: (i, 0)),
                  pl.BlockSpec((tm, half), lambda i: (i, 0)),
                  const(w_out.shape), const(ln_g.shape), const(ln_b.shape),
                  const(w_router.shape), const(b_router.shape), const(tri.shape)],
        out_specs=(pl.BlockSpec((tm, d_model), lambda i: (i, 0)),
                   pl.BlockSpec((tm, d_model // 2), lambda i: (i, 0)),
                   sel(), sel(), sel(),
                   pl.BlockSpec((None, N_EXPERTS, LANES), lambda i: (i, 0, 0))),
        compiler_params=_cparams(("parallel",)),
        name="mix_ln_router",
    )(x2d, attn, glu, w_out, ln_g, ln_b, w_router, b_router, tri)


def _dispatch_plan(eid, rank, hist, n_tok):
    bm = EXPERT_BLOCK
    n_assign = TOP_K * n_tok
    tm = TOKEN_TILE
    n_tiles = n_tok // tm
    counts = jnp.sum(hist, axis=0)
    ends = jnp.cumsum(counts)
    off = ends - counts
    base = off[None, :] + jnp.cumsum(hist, axis=0) - hist
    eid_t = eid.reshape(TOP_K, n_tiles, tm)
    onehot = eid_t[..., None] == jnp.arange(N_EXPERTS, dtype=jnp.int32)
    slot = rank.reshape(TOP_K, n_tiles, tm) + jnp.sum(
        jnp.where(onehot, base[None, :, None, :], 0), axis=-1)
    slot = slot.astype(jnp.int32)

    n_mtiles = n_assign // bm
    n_work = n_mtiles + N_EXPERTS
    t_lo = off // bm
    n_vis = jnp.where(counts > 0, (ends + bm - 1) // bm - t_lo, 0)
    w_end = jnp.cumsum(n_vis)
    w_start = w_end - n_vis
    n_valid = w_end[-1]
    w = jnp.arange(n_work, dtype=jnp.int32)
    wc = jnp.minimum(w, n_valid - 1)
    e_w = jnp.minimum(jnp.sum((w_end[None, :] <= wc[:, None]).astype(jnp.int32), axis=1),
                      N_EXPERTS - 1)
    pick = e_w[:, None] == jnp.arange(N_EXPERTS, dtype=jnp.int32)[None, :]
    take = lambda t: jnp.sum(jnp.where(pick, t[None, :], 0), axis=1)
    tile_w = (take(t_lo) + (wc - take(w_start))).astype(jnp.int32)
    valid = w < n_valid
    lo_w = jnp.where(valid, take(off), 0).astype(jnp.int32)
    hi_w = jnp.where(valid, take(ends), 0).astype(jnp.int32)
    prev_tile = jnp.concatenate([jnp.full((1,), -1, jnp.int32), tile_w[:-1]])
    first_w = (tile_w != prev_tile).astype(jnp.int32)
    prev_e = jnp.concatenate([jnp.full((1,), -1, jnp.int32), e_w[:-1]])
    newexp_w = (e_w != prev_e).astype(jnp.int32)
    work = dict(tile=tile_w, expert=e_w, lo=lo_w, hi=hi_w, first=first_w, newexp=newexp_w,
                valid=valid.astype(jnp.int32))
    return slot, work, n_work


def _dispatch_kernel(s0_ref, s1_ref, p0_ref, p1_ref, hp_ref, xs_hbm, ring, sem):
    i = pl.program_id(0)
    n_steps = pl.num_programs(0)
    tm = hp_ref.shape[0]
    cur = lax.rem(i, 2)

    def row_copy(r, s_ref, which):
        return pltpu.make_async_copy(ring.at[which, pl.ds(r, 1), :],
                                     xs_hbm.at[pl.ds(s_ref[0, r], 1), :], sem.at[which])

    def wait_tile(refs, which):
        for r in range(tm):
            for s_ref in refs:
                row_copy(r, s_ref, which).wait()

    ring[cur] = hp_ref[...]
    for r in range(tm):
        row_copy(r, s0_ref, cur).start()
        row_copy(r, s1_ref, cur).start()

    @pl.when(i >= 1)
    def _():
        wait_tile((p0_ref, p1_ref), 1 - cur)

    @pl.when(i == n_steps - 1)
    def _():
        wait_tile((s0_ref, s1_ref), cur)


def _dispatch(hpack, slot3):
    n_tok, words = hpack.shape
    tm = TOKEN_TILE
    n_tiles = n_tok // tm
    smem = lambda k, d: pl.BlockSpec(
        (None, 1, tm), lambda i: (jnp.maximum(i - d, 0) + k * n_tiles, 0, 0),
        memory_space=pltpu.SMEM)
    return pl.pallas_call(
        _dispatch_kernel,
        out_shape=jax.ShapeDtypeStruct((TOP_K * n_tok, words), U32),
        grid=(n_tiles,),
        in_specs=[smem(0, 0), smem(1, 0), smem(0, 1), smem(1, 1),
                  pl.BlockSpec((tm, words), lambda i: (i, 0))],
        out_specs=pl.BlockSpec(memory_space=pl.ANY),
        scratch_shapes=[pltpu.VMEM((2, tm, words), U32),
                        pltpu.SemaphoreType.DMA((2,))],
        compiler_params=_cparams(("arbitrary",)),
        name="moe_dispatch",
    )(slot3, slot3, slot3, slot3, hpack)


def _expert_kernel(tile_ref, exp_ref, lo_ref, hi_ref, first_ref, newexp_ref, valid_ref,
                   x_ref, wg_ref, wu_ref, wd_ref, o_ref, wg_bf, wu_bf, wd_bf):
    w = pl.program_id(0)
    bm, words = x_ref.shape

    @pl.when(valid_ref[w] == 1)
    def _():
        @pl.when(newexp_ref[w] == 1)
        def _():
            wg_bf[...] = wg_ref[...].astype(BF16)
            wu_bf[...] = wu_ref[...].astype(BF16)
            wd_bf[...] = wd_ref[...].astype(BF16)

        x_lo, x_hi = _unpack_bf16_pair(x_ref[...])
        x_lo, x_hi = x_lo.astype(BF16), x_hi.astype(BF16)
        gate = (jnp.dot(x_lo, wg_bf[0:words, :], preferred_element_type=F32)
                + jnp.dot(x_hi, wg_bf[words:, :], preferred_element_type=F32))
        up = (jnp.dot(x_lo, wu_bf[0:words, :], preferred_element_type=F32)
              + jnp.dot(x_hi, wu_bf[words:, :], preferred_element_type=F32))
        hid = (jax.nn.silu(gate) * up).astype(BF16)
        y = jnp.dot(hid, wd_bf[...], preferred_element_type=F32)
        packed = _pack_bf16_pair(y[:, :words], y[:, words:])
        row = tile_ref[w] * bm + lax.broadcasted_iota(jnp.int32, (bm, words), 0)
        mine = jnp.logical_and(row >= lo_ref[w], row < hi_ref[w])

        @pl.when(first_ref[w] == 1)
        def _():
            o_ref[...] = jnp.where(mine, packed, jnp.zeros_like(packed))

        @pl.when(first_ref[w] == 0)
        def _():
            o_ref[...] = jnp.where(mine, packed, o_ref[...])


def _experts(xs, work, n_work, w_gate, w_up, w_down):
    n_rows, words = xs.shape
    d_model, d_exp = w_gate.shape[1], w_gate.shape[2]
    bm = EXPERT_BLOCK
    names = ("tile", "expert", "lo", "hi", "first", "newexp", "valid")
    n_pre = len(names)
    row_map = lambda w, *pre: (pre[0][w], 0)
    exp_map = lambda w, *pre: (pre[1][w], 0, 0)
    grid_spec = pltpu.PrefetchScalarGridSpec(
        num_scalar_prefetch=n_pre,
        grid=(n_work,),
        in_specs=[pl.BlockSpec((bm, words), row_map),
                  pl.BlockSpec((None, d_model, d_exp), exp_map),
                  pl.BlockSpec((None, d_model, d_exp), exp_map),
                  pl.BlockSpec((None, d_exp, d_model), exp_map)],
        out_specs=pl.BlockSpec((bm, words), row_map),
        scratch_shapes=[pltpu.VMEM((d_model, d_exp), BF16),
                        pltpu.VMEM((d_model, d_exp), BF16),
                        pltpu.VMEM((d_exp, d_model), BF16)])
    return pl.pallas_call(
        _expert_kernel,
        out_shape=jax.ShapeDtypeStruct((n_rows, words), U32),
        grid_spec=grid_spec,
        compiler_params=_cparams(("arbitrary",)),
        name="expert_ffn",
    )(*[work[n] for n in names], xs, w_gate, w_up, w_down)


def _final_kernel(s0_ref, s1_ref, n0_ref, n1_ref, h_ref, gate_ref, g_ref, b_ref, ys_hbm,
                  o_ref, buf, sem):
    i = pl.program_id(0)
    n_steps = pl.num_programs(0)
    tm = h_ref.shape[0]
    cur = lax.rem(i, 2)

    def row_copy(r, k, s_ref, which):
        return pltpu.make_async_copy(ys_hbm.at[pl.ds(s_ref[0, r], 1), :],
                                     buf.at[which, k, pl.ds(r, 1), :], sem.at[which])

    def start_tile(refs, which):
        for r in range(tm):
            for k in range(TOP_K):
                row_copy(r, k, refs[k], which).start()

    def wait_tile(refs, which):
        for r in range(tm):
            for k in range(TOP_K):
                row_copy(r, k, refs[k], which).wait()

    @pl.when(i == 0)
    def _():
        start_tile((s0_ref, s1_ref), 0)

    wait_tile((s0_ref, s1_ref), cur)
    rows_per = tm // COMBINE_CHUNKS
    for c in range(COMBINE_CHUNKS):
        rs = slice(c * rows_per, (c + 1) * rows_per)
        gate = gate_ref[rs, :]
        y0_lo, y0_hi = _unpack_bf16_pair(buf[cur, 0, rs, :])
        y1_lo, y1_hi = _unpack_bf16_pair(buf[cur, 1, rs, :])
        g0, g1 = gate[:, 0:1], gate[:, 1:2]
        y = jnp.concatenate([y0_lo * g0 + y1_lo * g1, y0_hi * g0 + y1_hi * g1], axis=-1)
        o_ref[rs, :] = _layer_norm(ALPHA * h_ref[rs, :] + y, g_ref[...], b_ref[...])
        for r in range(c * rows_per, (c + 1) * rows_per):
            for k in range(TOP_K):
                row_copy(r, k, (n0_ref, n1_ref)[k], 1 - cur).start()

    @pl.when(i == n_steps - 1)
    def _():
        wait_tile((n0_ref, n1_ref), 1 - cur)


def _final(h1, ys, slot3, gate_t, ln_g, ln_b):
    n_tok, d_model = h1.shape
    words = ys.shape[1]
    tm = TOKEN_TILE
    n_tiles = n_tok // tm
    const = lambda shape: pl.BlockSpec(shape, lambda i: (0,) * len(shape))
    cur = lambda k: pl.BlockSpec((None, 1, tm), lambda i: (i + k * n_tiles, 0, 0),
                                 memory_space=pltpu.SMEM)
    nxt = lambda k: pl.BlockSpec(
        (None, 1, tm), lambda i: (jnp.minimum(i + 1, n_tiles - 1) + k * n_tiles, 0, 0),
        memory_space=pltpu.SMEM)
    return pl.pallas_call(
        _final_kernel,
        out_shape=jax.ShapeDtypeStruct((n_tok, d_model), F32),
        grid=(n_tiles,),
        in_specs=[cur(0), cur(1), nxt(0), nxt(1),
                  pl.BlockSpec((tm, d_model), lambda i: (i, 0)),
                  pl.BlockSpec((tm, TOP_K), lambda i: (i, 0)),
                  const(ln_g.shape), const(ln_b.shape),
                  pl.BlockSpec(memory_space=pl.ANY)],
        out_specs=pl.BlockSpec((tm, d_model), lambda i: (i, 0)),
        scratch_shapes=[pltpu.VMEM((2, TOP_K, tm, words), U32),
                        pltpu.SemaphoreType.DMA((2,))],
        compiler_params=_cparams(("arbitrary",)),
        name="combine_ln",
    )(slot3, slot3, slot3, slot3, h1, gate_t, ln_g, ln_b, ys)


def kernel(x, w_in, rpb, s5_a_re, s5_a_im, s5_log_dt, s5_b_re, s5_b_im, s5_c_re, s5_c_im, s5_d, w_glu, b_glu, w_out, ln1_g, ln1_b, w_router_group, b_router_group, w_router_expert, b_router_expert, w_gate, w_up, w_down, ln2_g, ln2_b):
    bsz, seq, d_model = x.shape
    n_tok = bsz * seq
    assert bsz == SUBLANES, "the S5 chunk scan keeps one batch element per sublane"
    assert seq % (ATTN_ROWS_PER_STEP * GRID_W) == 0 and n_tok % TOKEN_TILE == 0
    assert (TOP_K * n_tok) % EXPERT_BLOCK == 0
    assert (seq // S5_CHUNK) % LANES == 0, "each batch element owns whole lane tiles of chunks"
    assert w_in.shape[0] == DEPTH
    layer = 0
    x2d = x.reshape(n_tok, d_model)
    width = w_in.shape[-1] // 4
    n_heads = width // HEAD_DIM
    n_grp = s5_d.shape[1]

    w_in_bf = w_in[layer].astype(BF16)
    q, k, v = _in_proj(x2d, w_in_bf[:, :3 * width])
    ut = _u_proj(x2d, jnp.transpose(w_in_bf[:, 3 * width:]), n_grp)

    shape3 = (bsz, seq, width)
    attn = _attention(q.reshape(shape3), k.reshape(shape3), v.reshape(shape3),
                      _attn_bias_table(rpb[layer]), n_heads).reshape(n_tok, width)

    n_chunks = seq // S5_CHUNK
    lc = S5_CHUNK * S5_GROUP_CH
    w_mat, t_t, v_t, lam_re, lam_im = _s5_tables(
        s5_a_re[layer], s5_a_im[layer], s5_log_dt[layer], s5_b_re[layer], s5_b_im[layer],
        s5_c_re[layer], s5_c_im[layer], s5_d[layer])
    yt = _s5(ut.reshape(n_grp, lc, bsz * n_chunks), w_mat, t_t, v_t, lam_re, lam_im,
             bsz, n_chunks)
    glu = _glu(yt, jnp.transpose(w_glu[layer]).astype(BF16),
               b_glu[layer].astype(F32).reshape(-1, 1))

    n_router = N_EXPERT_GROUPS + N_EXPERTS
    w_router = jnp.concatenate([w_router_group[layer], w_router_expert[layer]], axis=1)
    w_router = jnp.pad(jnp.transpose(w_router), ((0, LANES - n_router), (0, 0))).astype(BF16)
    b_router = jnp.concatenate([b_router_group[layer], b_router_expert[layer]])
    b_router = jnp.pad(b_router, (0, LANES - n_router)).astype(F32).reshape(LANES, 1)
    row = lambda t: t.astype(F32).reshape(1, -1)
    h1, hpack, eid, gate, rank, hist = _mix(
        x2d, attn, glu, w_out[layer].astype(BF16), row(ln1_g[layer]), row(ln1_b[layer]),
        w_router, b_router)

    slot, work, n_work = _dispatch_plan(eid, rank, hist[:, :, 0], n_tok)
    slot3 = slot.reshape(TOP_K * (n_tok // TOKEN_TILE), 1, TOKEN_TILE)
    xs = _dispatch(hpack, slot3)
    ys = _experts(xs, work, n_work, w_gate[layer], w_up[layer], w_down[layer])

    out = _final(h1, ys, slot3, jnp.transpose(gate), row(ln2_g[layer]), row(ln2_b[layer]))
    return out.reshape(bsz, seq, d_model)
```

```python
import functools
import math

import numpy as np
import jax
import jax.numpy as jnp
from jax import lax
from jax.experimental import pallas as pl
from jax.experimental.pallas import tpu as pltpu

F32 = jnp.float32
BF16 = jnp.bfloat16
U32 = jnp.uint32

GRID_W = 64
WIN_H = 8
WIN_W = 16
HEAD_DIM = 64
N_EXPERT_GROUPS = 8
EXPERTS_PER_GROUP = 8
N_EXPERTS = N_EXPERT_GROUPS * EXPERTS_PER_GROUP
TOP_K = 2
S5_GROUP_CH = 16
S5_STATE = 64
LN_EPS = 1e-5
DEPTH = 1
ALPHA = (2.0 * DEPTH) ** 0.25

LANES = 128
SUBLANES = 8
VMEM_LIMIT_BYTES = 56 * 1024 * 1024

TOKEN_TILE = 512
S5_CHUNK = 16
S5_TOKEN_TILE = S5_CHUNK * LANES
S5_SWAP_ROWS = 64
ATTN_ROWS_PER_STEP = 8
EXPERT_BLOCK = 512
COMBINE_CHUNKS = 16
NEG_BIAS = -1e30


def _cparams(sem):
    return pltpu.CompilerParams(dimension_semantics=sem,
                                vmem_limit_bytes=VMEM_LIMIT_BYTES)


def _inproj_kernel(x_ref, w_ref, q_ref, k_ref, v_ref, *, width, scale):
    x = x_ref[...].astype(BF16)
    outs = (q_ref, k_ref, v_ref)
    for i, o_ref in enumerate(outs):
        acc = jnp.dot(x, w_ref[:, i * width:(i + 1) * width],
                      preferred_element_type=F32)
        if i == 0:
            acc = acc * scale
        o_ref[...] = acc.astype(BF16)


def _in_proj(x2d, w_qkv_bf):
    n_tok, d_model = x2d.shape
    width = w_qkv_bf.shape[1] // 3
    tm = TOKEN_TILE
    out = jax.ShapeDtypeStruct((n_tok, width), BF16)
    o_spec = pl.BlockSpec((tm, width), lambda i: (i, 0))
    return pl.pallas_call(
        functools.partial(_inproj_kernel, width=width, scale=HEAD_DIM ** -0.5),
        out_shape=(out, out, out),
        grid=(n_tok // tm,),
        in_specs=[pl.BlockSpec((tm, d_model), lambda i: (i, 0)),
                  pl.BlockSpec((d_model, 3 * width), lambda i: (0, 0))],
        out_specs=(o_spec, o_spec, o_spec),
        compiler_params=_cparams(("parallel",)),
        name="in_proj",
    )(x2d, w_qkv_bf)


def _uproj_kernel(x_ref, w_ref, o_ref):
    tm, d_model = x_ref.shape
    nc = tm // S5_CHUNK
    xp = pltpu.einshape("jtd->tjd", x_ref[...].reshape(nc, S5_CHUNK, d_model))
    xp = xp.reshape(tm, d_model).astype(BF16)
    ut = lax.dot_general(w_ref[...], xp, (((1,), (1,)), ((), ())),
                         preferred_element_type=F32)
    n_grp = o_ref.shape[0]
    for s in range(S5_CHUNK):
        o_ref[:, s, :, :] = ut[:, s * nc:(s + 1) * nc].astype(BF16).reshape(n_grp, S5_GROUP_CH, nc)


def _u_proj(x2d, w_u_t_bf, n_grp):
    n_tok, d_model = x2d.shape
    tm = S5_TOKEN_TILE
    nc = tm // S5_CHUNK
    return pl.pallas_call(
        _uproj_kernel,
        out_shape=jax.ShapeDtypeStruct((n_grp, S5_CHUNK, S5_GROUP_CH, n_tok // S5_CHUNK), BF16),
        grid=(n_tok // tm,),
        in_specs=[pl.BlockSpec((tm, d_model), lambda i: (i, 0)),
                  pl.BlockSpec(w_u_t_bf.shape, lambda i: (0, 0))],
        out_specs=pl.BlockSpec((n_grp, S5_CHUNK, S5_GROUP_CH, nc), lambda i: (0, 0, 0, i)),
        compiler_params=_cparams(("parallel",)),
        name="u_proj_t",
    )(x2d, w_u_t_bf)


def _attn_bias_table(rpb):
    col = np.arange(GRID_W)
    cstart = np.clip(col - WIN_W // 2, 0, GRID_W - WIN_W)
    j = np.arange(GRID_W)
    valid = (j[None, :] >= cstart[:, None]) & (j[None, :] < cstart[:, None] + WIN_W)
    dcol = j[None, :] - col[:, None] + (WIN_W - 1)
    pick = (dcol[..., None] == np.arange(2 * WIN_W - 1)) & valid[..., None]
    full = jnp.einsum('hdm,cjm->hdcj', rpb.astype(F32), jnp.asarray(pick, F32),
                      precision=lax.Precision.HIGHEST)
    full = jnp.where(jnp.asarray(valid)[None, None], full, NEG_BIAS)
    pair = jnp.concatenate([full[:, :-1], full[:, 1:]], axis=-1)
    n_heads = pair.shape[0]
    pair = pair.reshape(n_heads // 2, 2, 2 * WIN_H - 2, GRID_W, LANES)
    return jnp.transpose(pair, (2, 0, 1, 3, 4)).reshape(2 * WIN_H - 2, n_heads // 2,
                                                       2 * GRID_W, LANES)


def _attn_kernel(q_ref, k_ref, v_ref, b_ref, o_ref, *, n_rows, n_heads):
    rblk = pl.program_id(1)
    band = WIN_H * GRID_W
    n_pairs = n_heads // 2
    lane = lax.broadcasted_iota(jnp.int32, (GRID_W, LANES), 1)
    low_half = lane < HEAD_DIM

    def row_body(rr, carry):
        r = rblk * ATTN_ROWS_PER_STEP + rr
        rs = jnp.clip(r - WIN_H // 2, 0, n_rows - WIN_H)
        shift = rs - r + (WIN_H - 1)
        q_off = pl.multiple_of(rr * GRID_W, GRID_W)
        k_off = pl.multiple_of(rs * GRID_W, GRID_W)
        scores = []
        for hp in range(n_pairs):
            cols = slice(hp * LANES, (hp + 1) * LANES)
            qp = q_ref[pl.ds(q_off, GRID_W), cols]
            zero = jnp.zeros_like(qp)
            q2 = jnp.concatenate([jnp.where(low_half, qp, zero), jnp.where(low_half, zero, qp)],
                                 axis=0)
            kp = k_ref[pl.ds(k_off, band), cols]
            scores.append(lax.dot_general(q2, kp, (((1,), (1,)), ((), ())),
                                          preferred_element_type=F32))
        probs, dens = [], []
        for hp in range(n_pairs):
            s = jnp.concatenate(
                [scores[hp][:, m * LANES:(m + 1) * LANES] + b_ref[shift + 2 * m, hp]
                 for m in range(band // LANES)], axis=-1)
            mx = jnp.max(s, axis=-1, keepdims=True)
            p = jnp.exp(s - mx)
            dens.append(jnp.sum(p, axis=-1, keepdims=True))
            probs.append(p.astype(BF16))
        for hp in range(n_pairs):
            cols = slice(hp * LANES, (hp + 1) * LANES)
            vp = v_ref[pl.ds(k_off, band), cols]
            o = jnp.dot(probs[hp], vp, preferred_element_type=F32) / dens[hp]
            o_ref[pl.ds(q_off, GRID_W), cols] = jnp.where(
                low_half, o[:GRID_W], o[GRID_W:]).astype(BF16)
        return carry

    lax.fori_loop(0, ATTN_ROWS_PER_STEP, row_body, 0, unroll=True)


def _attention(q, k, v, bias_tab, n_heads):
    bsz, seq, width = q.shape
    n_rows = seq // GRID_W
    tq = ATTN_ROWS_PER_STEP * GRID_W
    return pl.pallas_call(
        functools.partial(_attn_kernel, n_rows=n_rows, n_heads=n_heads),
        out_shape=jax.ShapeDtypeStruct((bsz, seq, width), BF16),
        grid=(bsz, seq // tq),
        in_specs=[pl.BlockSpec((None, tq, width), lambda b, r: (b, r, 0)),
                  pl.BlockSpec((None, seq, width), lambda b, r: (b, 0, 0)),
                  pl.BlockSpec((None, seq, width), lambda b, r: (b, 0, 0)),
                  pl.BlockSpec(bias_tab.shape, lambda b, r: (0, 0, 0, 0))],
        out_specs=pl.BlockSpec((None, tq, width), lambda b, r: (b, r, 0)),
        compiler_params=_cparams(("parallel", "arbitrary")),
        name="nbr_attention",
    )(q, k, v, bias_tab)


def _s5_tables(a_re, a_im, log_dt, b_re, b_im, c_re, c_im, d_skip):
    hi = lax.Precision.HIGHEST
    L = S5_CHUNK
    n_grp, n_state = a_re.shape[1], a_re.shape[2]
    n_ch = b_re.shape[-1]
    f = lambda t: t.astype(F32)
    ks = jnp.arange(L + 1, dtype=F32)

    t_dirs, w_re, w_im, v_re, v_im, lam_l_re, lam_l_im = [], [], [], [], [], [], []
    for direction in range(2):
        ar, ai = f(a_re[direction]), f(a_im[direction])
        dt = jnp.exp(f(log_dt[direction]))[:, None]
        mag = jnp.exp(ar * dt)
        lr, li = mag * jnp.cos(ai * dt), mag * jnp.sin(ai * dt)
        den = ar * ar + ai * ai
        zr = ((lr - 1.0) * ar + li * ai) / den
        zi = (li * ar - (lr - 1.0) * ai) / den
        br, bi = f(b_re[direction]), f(b_im[direction])
        bbr = zr[..., None] * br - zi[..., None] * bi
        bbi = zr[..., None] * bi + zi[..., None] * br
        cr, ci = f(c_re[direction]), f(c_im[direction])
        pmag = jnp.exp(ks[:, None, None] * (ar * dt)[None])
        pr = pmag * jnp.cos(ks[:, None, None] * (ai * dt)[None])
        pi = pmag * jnp.sin(ks[:, None, None] * (ai * dt)[None])
        lbr = pr[..., None] * bbr[None] - pi[..., None] * bbi[None]
        lbi = pr[..., None] * bbi[None] + pi[..., None] * bbr[None]
        c_cat = jnp.concatenate([cr, -ci], axis=-1)
        lb_cat = jnp.concatenate([lbr, lbi], axis=2)
        lb_cat = jnp.transpose(lb_cat, (1, 2, 0, 3)).reshape(n_grp, 2 * n_state, (L + 1) * n_ch)
        kern = jnp.einsum('gcq,gqn->gcn', c_cat, lb_cat, precision=hi)
        kern = jnp.transpose(kern.reshape(n_grp, n_ch, L + 1, n_ch), (2, 0, 1, 3))
        s_idx = np.arange(L)[:, None]
        t_idx = np.arange(L)[None, :]
        lag = (t_idx - s_idx) if direction == 0 else (s_idx - t_idx)
        lag_pick = jnp.asarray(lag[..., None] == np.arange(L + 1), F32)
        tk = jnp.einsum('stk,kgcd->stgcd', lag_pick, kern, precision=hi)
        t_dirs.append(jnp.transpose(tk, (2, 0, 4, 1, 3)))
        lb_w = (lambda t: jnp.flip(t[:L], 0)) if direction == 0 else (lambda t: t[:L])
        w_re.append(jnp.transpose(lb_w(lbr), (1, 0, 3, 2)))
        w_im.append(jnp.transpose(lb_w(lbi), (1, 0, 3, 2)))
        p_o = (lambda t: t[1:]) if direction == 0 else (lambda t: jnp.flip(t[1:], 0))
        pro, pio = p_o(pr), p_o(pi)
        clr = cr[None] * pro[:, :, None, :] - ci[None] * pio[:, :, None, :]
        cli = cr[None] * pio[:, :, None, :] + ci[None] * pro[:, :, None, :]
        v_re.append(jnp.transpose(clr, (1, 3, 0, 2)))
        v_im.append(jnp.transpose(-cli, (1, 3, 0, 2)))
        lam_l_re.append(pr[L])
        lam_l_im.append(pi[L])

    lc = L * n_ch
    eye = jnp.eye(lc, dtype=F32).reshape(L, n_ch, L, n_ch)
    toep = t_dirs[0] + t_dirs[1] + eye[None] * f(d_skip)[:, None, None, None, :]
    toep = toep.reshape(n_grp, lc, lc)
    w_all = jnp.concatenate([w_re[0], w_re[1], w_im[0], w_im[1]], axis=-1)
    w_mat = w_all.reshape(n_grp, lc, 4 * n_state)
    v_mat = jnp.concatenate([v_re[0], v_re[1], v_im[0], v_im[1]], axis=1)
    v_mat = v_mat.reshape(n_grp, 4 * n_state, lc)
    lam_re = jnp.concatenate([lam_l_re[0], lam_l_re[1]], axis=-1)
    lam_im = jnp.concatenate([lam_l_im[0], lam_l_im[1]], axis=-1)
    bc = lambda t: jnp.broadcast_to(t[:, None, :], (n_grp, SUBLANES, 2 * n_state))
    return (w_mat.astype(BF16), jnp.swapaxes(toep, 1, 2).astype(BF16),
            jnp.swapaxes(v_mat, 1, 2).astype(BF16), bc(lam_re), bc(lam_im))


def _gelu_tanh(x):
    cdf = 0.5 * (1.0 + jnp.tanh(math.sqrt(2.0 / math.pi) * (x + 0.044715 * (x * x * x))))
    return x * cdf


def _s5_kernel(ut_ref, w_ref, tt_ref, vt_ref, lr_ref, li_ref, o_ref, xn_scr, x_scr, h_scr, hn_scr,
               *, n_chunks, n_state):
    n_batch = xn_scr.shape[0]
    sw = S5_SWAP_ROWS
    two_p = 2 * n_state

    for b in range(n_batch):
        cols = slice(b * n_chunks, (b + 1) * n_chunks)
        xn_scr[b] = lax.dot_general(ut_ref[:, cols], w_ref[...], (((0,), (0,)), ((), ())),
                                    preferred_element_type=F32)
    for jb in range(n_chunks // sw):
        rows = slice(jb * sw, (jb + 1) * sw)
        x_scr[rows] = pltpu.einshape("bjl->jbl", xn_scr[:, rows, :])

    lam_re = lr_ref[...]
    lam_im = li_ref[...]
    is_fwd = lax.broadcasted_iota(jnp.int32, (n_batch, two_p), 1) < n_state

    def scan_body(j, carry):
        h_re, h_im = carry
        jr = n_chunks - 1 - j
        x_re = jnp.where(is_fwd, x_scr[j, :, 0:two_p], x_scr[jr, :, 0:two_p])
        x_im = jnp.where(is_fwd, x_scr[j, :, two_p:2 * two_p], x_scr[jr, :, two_p:2 * two_p])
        h_scr[j, :, 0:n_state] = h_re[:, 0:n_state]
        h_scr[jr, :, n_state:two_p] = h_re[:, n_state:two_p]
        h_scr[j, :, two_p:two_p + n_state] = h_im[:, 0:n_state]
        h_scr[jr, :, two_p + n_state:2 * two_p] = h_im[:, n_state:two_p]
        n_re = lam_re * h_re - lam_im * h_im + x_re
        n_im = lam_re * h_im + lam_im * h_re + x_im
        return n_re, n_im

    zero = jnp.zeros((n_batch, two_p), F32)
    lax.fori_loop(0, n_chunks, scan_body, (zero, zero))

    for jb in range(n_chunks // sw):
        rows = slice(jb * sw, (jb + 1) * sw)
        hn_scr[:, rows, :] = pltpu.einshape("jbl->bjl", h_scr[rows])
    for b in range(n_batch):
        cols = slice(b * n_chunks, (b + 1) * n_chunks)
        y = (jnp.dot(tt_ref[...], ut_ref[:, cols], preferred_element_type=F32)
             + lax.dot_general(vt_ref[...], hn_scr[b].astype(BF16), (((1,), (1,)), ((), ())),
                               preferred_element_type=F32))
        o_ref[:, cols] = _gelu_tanh(y).astype(BF16)


def _s5(ut, w_mat, t_t, v_t, lam_re, lam_im, n_batch, n_chunks):
    n_grp, lc, n_cols = ut.shape
    four_p = w_mat.shape[2]
    n_state = four_p // 4
    grp = lambda shape: pl.BlockSpec((None,) + shape, lambda g: (g, 0, 0))
    return pl.pallas_call(
        functools.partial(_s5_kernel, n_chunks=n_chunks, n_state=n_state),
        out_shape=jax.ShapeDtypeStruct((n_grp, lc, n_cols), BF16),
        grid=(n_grp,),
        in_specs=[grp((lc, n_cols)), grp((lc, four_p)), grp((lc, lc)), grp((lc, four_p)),
                  grp((SUBLANES, 2 * n_state)), grp((SUBLANES, 2 * n_state))],
        out_specs=grp((lc, n_cols)),
        scratch_shapes=[pltpu.VMEM((n_batch, n_chunks, four_p), F32),
                        pltpu.VMEM((n_chunks, n_batch, four_p), F32),
                        pltpu.VMEM((n_chunks, n_batch, four_p), F32),
                        pltpu.VMEM((n_batch, n_chunks, four_p), F32)],
        compiler_params=_cparams(("parallel",)),
        name="s5_chunked",
    )(ut, w_mat, t_t, v_t, lam_re, lam_im)


def _glu_kernel(yt_ref, wt_ref, b_ref, o_ref):
    n_grp, lc, nc = yt_ref.shape
    n_ch = lc // S5_CHUNK
    width = n_grp * n_ch
    ssm_t = jnp.concatenate(
        [yt_ref[:, t * n_ch:(t + 1) * n_ch, :].reshape(width, nc) for t in range(S5_CHUNK)], axis=1)
    z = jnp.dot(wt_ref[...], ssm_t, preferred_element_type=F32) + b_ref[...]
    glu_t = ssm_t.astype(F32) * jax.nn.sigmoid(z)
    glu = jnp.transpose(glu_t).reshape(S5_CHUNK, nc, width)
    o_ref[...] = pltpu.einshape("tjd->jtd", glu).reshape(S5_CHUNK * nc, width).astype(BF16)


def _glu(yt, w_glu_t_bf, b_glu_col):
    n_grp, lc, n_cols = yt.shape
    nc = LANES
    width = w_glu_t_bf.shape[0]
    return pl.pallas_call(
        _glu_kernel,
        out_shape=jax.ShapeDtypeStruct((n_cols * S5_CHUNK, width), BF16),
        grid=(n_cols // nc,),
        in_specs=[pl.BlockSpec((n_grp, lc, nc), lambda i: (0, 0, i)),
                  pl.BlockSpec(w_glu_t_bf.shape, lambda i: (0, 0)),
                  pl.BlockSpec(b_glu_col.shape, lambda i: (0, 0))],
        out_specs=pl.BlockSpec((nc * S5_CHUNK, width), lambda i: (i, 0)),
        compiler_params=_cparams(("parallel",)),
        name="glu_to_tokens",
    )(yt, w_glu_t_bf, b_glu_col)


def _layer_norm(x, g, b):
    mu = jnp.mean(x, axis=-1, keepdims=True)
    xc = x - mu
    var = jnp.mean(xc * xc, axis=-1, keepdims=True)
    return xc * lax.rsqrt(var + LN_EPS) * g + b


def _pack_bf16_pair(lo, hi):
    lo_bits = lax.bitcast_convert_type(lo.astype(BF16).astype(F32), U32) >> 16
    hi_bits = lax.bitcast_convert_type(hi.astype(BF16).astype(F32), U32) & jnp.uint32(0xFFFF0000)
    return lo_bits | hi_bits


def _unpack_bf16_pair(word):
    lo = lax.bitcast_convert_type(word << 16, F32)
    hi = lax.bitcast_convert_type(word & jnp.uint32(0xFFFF0000), F32)
    return lo, hi


def _first_index_of_max(vals, row_id, n):
    mx = jnp.max(vals, axis=0, keepdims=True)
    idx = jnp.min(jnp.where(vals == mx, row_id, n), axis=0, keepdims=True)
    return mx, idx


def _mix_kernel(x_ref, a_ref, s_ref, wout_ref, g_ref, b_ref,
                wr_ref, br_ref, tri_ref, h_ref, hp_ref, eid_ref, gate_ref, rank_ref, hist_ref):
    half = a_ref.shape[1]
    mix = (jnp.dot(a_ref[...], wout_ref[0:half, :], preferred_element_type=F32)
           + jnp.dot(s_ref[...], wout_ref[half:, :], preferred_element_type=F32))
    h = _layer_norm(ALPHA * x_ref[...] + mix, g_ref[...], b_ref[...])
    h_ref[...] = h
    d_half = h.shape[1] // 2
    hp_ref[...] = _pack_bf16_pair(h[:, :d_half], h[:, d_half:])

    logits = lax.dot_general(wr_ref[...], h.astype(BF16), (((1,), (1,)), ((), ())),
                             preferred_element_type=F32) + br_ref[...]
    ng, epg = N_EXPERT_GROUPS, EXPERTS_PER_GROUP
    tm = logits.shape[1]
    row_id = lax.broadcasted_iota(jnp.int32, (ng, tm), 0)
    g_logit = logits[0:ng, :]
    g_max, g_idx = _first_index_of_max(g_logit, row_id, ng)
    g_val = 1.0 / jnp.sum(jnp.exp(g_logit - g_max), axis=0, keepdims=True)
    e_in = jnp.zeros((epg, tm), F32)
    for gi in range(ng):
        e_in = jnp.where(g_idx == gi, logits[ng + gi * epg:ng + (gi + 1) * epg, :], e_in)
    m1, i1 = _first_index_of_max(e_in, row_id, epg)
    rest = jnp.where(row_id == i1, -jnp.inf, e_in)
    m2, i2 = _first_index_of_max(rest, row_id, epg)
    e2 = jnp.exp(m2 - m1)
    w1 = 1.0 / (1.0 + e2)
    w2 = e2 / (1.0 + e2)
    eids = (g_idx * epg + i1, g_idx * epg + i2)
    gate_ref[0:1, :] = g_val * w1
    gate_ref[1:2, :] = g_val * w2

    exp_id = lax.broadcasted_iota(jnp.int32, (N_EXPERTS, tm), 0)
    before = jnp.zeros((N_EXPERTS, 1), F32)
    for kk in range(TOP_K):
        onehot = exp_id == eids[kk]
        prefix = jnp.dot(onehot.astype(BF16), tri_ref[...], preferred_element_type=F32)
        rank = jnp.sum(jnp.where(onehot, prefix + before, 0.0), axis=0, keepdims=True)
        eid_ref[kk:kk + 1, :] = eids[kk]
        rank_ref[kk:kk + 1, :] = rank.astype(jnp.int32)
        before = before + jnp.sum(onehot.astype(F32), axis=1, keepdims=True)
    hist_ref[...] = jnp.broadcast_to(before, hist_ref.shape).astype(jnp.int32)


def _mix(x2d, attn, glu, w_out, ln_g, ln_b, w_router, b_router):
    n_tok, d_model = x2d.shape
    half = attn.shape[1]
    tm = TOKEN_TILE
    n_tiles = n_tok // tm
    tri = jnp.asarray(np.triu(np.ones((tm, tm), np.float32), k=1), dtype=BF16)
    const = lambda shape: pl.BlockSpec(shape, lambda i: (0,) * len(shape))
    sel = lambda: pl.BlockSpec((TOP_K, tm), lambda i: (0, i))
    return pl.pallas_call(
        _mix_kernel,
        out_shape=(jax.ShapeDtypeStruct((n_tok, d_model), F32),
                   jax.ShapeDtypeStruct((n_tok, d_model // 2), U32),
                   jax.ShapeDtypeStruct((TOP_K, n_tok), jnp.int32),
                   jax.ShapeDtypeStruct((TOP_K, n_tok), F32),
                   jax.ShapeDtypeStruct((TOP_K, n_tok), jnp.int32),
                   jax.ShapeDtypeStruct((n_tiles, N_EXPERTS, LANES), jnp.int32)),
        grid=(n_tiles,),
        in_specs=[pl.BlockSpec((tm, d_model), lambda i: (i, 0)),
                  pl.BlockSpec((tm, half), lambda i: (i, 0)),
                  pl.BlockSpec((tm, half), lambda i: (i, 0)),
                  const(w_out.shape), const(ln_g.shape), const(ln_b.shape),
                  const(w_router.shape), const(b_router.shape), const(tri.shape)],
        out_specs=(pl.BlockSpec((tm, d_model), lambda i: (i, 0)),
                   pl.BlockSpec((tm, d_model // 2), lambda i: (i, 0)),
                   sel(), sel(), sel(),
                   pl.BlockSpec((None, N_EXPERTS, LANES), lambda i: (i, 0, 0))),
        compiler_params=_cparams(("parallel",)),
        name="mix_ln_router",
    )(x2d, attn, glu, w_out, ln_g, ln_b, w_router, b_router, tri)


def _dispatch_plan(eid, rank, hist, n_tok):
    bm = EXPERT_BLOCK
    n_assign = TOP_K * n_tok
    tm = TOKEN_TILE
    n_tiles = n_tok // tm
    counts = jnp.sum(hist, axis=0)
    ends = jnp.cumsum(counts)
    off = ends - counts
    base = off[None, :] + jnp.cumsum(hist, axis=0) - hist
    eid_t = eid.reshape(TOP_K, n_tiles, tm)
    onehot = eid_t[..., None] == jnp.arange(N_EXPERTS, dtype=jnp.int32)
    slot = rank.reshape(TOP_K, n_tiles, tm) + jnp.sum(
        jnp.where(onehot, base[None, :, None, :], 0), axis=-1)
    slot = slot.astype(jnp.int32)

    n_mtiles = n_assign // bm
    n_work = n_mtiles + N_EXPERTS
    t_lo = off // bm
    n_vis = jnp.where(counts > 0, (ends + bm - 1) // bm - t_lo, 0)
    w_end = jnp.cumsum(n_vis)
    w_start = w_end - n_vis
    n_valid = w_end[-1]
    w = jnp.arange(n_work, dtype=jnp.int32)
    wc = jnp.minimum(w, n_valid - 1)
    e_w = jnp.minimum(jnp.sum((w_end[None, :] <= wc[:, None]).astype(jnp.int32), axis=1),
                      N_EXPERTS - 1)
    pick = e_w[:, None] == jnp.arange(N_EXPERTS, dtype=jnp.int32)[None, :]
    take = lambda t: jnp.sum(jnp.where(pick, t[None, :], 0), axis=1)
    tile_w = (take(t_lo) + (wc - take(w_start))).astype(jnp.int32)
    valid = w < n_valid
    lo_w = jnp.where(valid, take(off), 0).astype(jnp.int32)
    hi_w = jnp.where(valid, take(ends), 0).astype(jnp.int32)
    prev_tile = jnp.concatenate([jnp.full((1,), -1, jnp.int32), tile_w[:-1]])
    first_w = (tile_w != prev_tile).astype(jnp.int32)
    prev_e = jnp.concatenate([jnp.full((1,), -1, jnp.int32), e_w[:-1]])
    newexp_w = (e_w != prev_e).astype(jnp.int32)
    k_w = jnp.cumsum(newexp_w) - 1
    present = counts > 0
    pos = jnp.cumsum(present.astype(jnp.int32)) - 1
    n_seq = jnp.sum(present.astype(jnp.int32))
    ids = jnp.arange(N_EXPERTS, dtype=jnp.int32)
    seq = jnp.sum(jnp.where(present[None, :] & (pos[None, :] == ids[:, None]), ids[None, :], 0),
                  axis=1)
    take_seq = lambda idx: jnp.sum(
        jnp.where(jnp.minimum(idx, n_seq - 1)[:, None] == ids[None, :], seq[None, :], 0), axis=1)
    work = dict(tile=tile_w, wa=take_seq(2 * ((k_w + 1) // 2)).astype(jnp.int32),
                wb=take_seq(2 * (k_w // 2) + 1).astype(jnp.int32),
                usea=(1 - k_w % 2).astype(jnp.int32), lo=lo_w, hi=hi_w, first=first_w,
                newexp=newexp_w, valid=valid.astype(jnp.int32))
    return slot, work, n_work


def _dispatch_kernel(s0_ref, s1_ref, p0_ref, p1_ref, hp_ref, xs_hbm, ring, sem):
    i = pl.program_id(0)
    n_steps = pl.num_programs(0)
    tm = hp_ref.shape[0]
    cur = lax.rem(i, 2)

    def row_copy(r, s_ref, which):
        return pltpu.make_async_copy(ring.at[which, pl.ds(r, 1), :],
                                     xs_hbm.at[pl.ds(s_ref[0, r], 1), :], sem.at[which])

    def wait_tile(refs, which):
        for r in range(tm):
            for s_ref in refs:
                row_copy(r, s_ref, which).wait()

    ring[cur] = hp_ref[...]
    for r in range(tm):
        row_copy(r, s0_ref, cur).start()
        row_copy(r, s1_ref, cur).start()

    @pl.when(i >= 1)
    def _():
        wait_tile((p0_ref, p1_ref), 1 - cur)

    @pl.when(i == n_steps - 1)
    def _():
        wait_tile((s0_ref, s1_ref), cur)


def _dispatch(hpack, slot3):
    n_tok, words = hpack.shape
    tm = TOKEN_TILE
    n_tiles = n_tok // tm
    smem = lambda k, d: pl.BlockSpec(
        (None, 1, tm), lambda i: (jnp.maximum(i - d, 0) + k * n_tiles, 0, 0),
        memory_space=pltpu.SMEM)
    return pl.pallas_call(
        _dispatch_kernel,
        out_shape=jax.ShapeDtypeStruct((TOP_K * n_tok, words), U32),
        grid=(n_tiles,),
        in_specs=[smem(0, 0), smem(1, 0), smem(0, 1), smem(1, 1),
                  pl.BlockSpec((tm, words), lambda i: (i, 0))],
        out_specs=pl.BlockSpec(memory_space=pl.ANY),
        scratch_shapes=[pltpu.VMEM((2, tm, words), U32),
                        pltpu.SemaphoreType.DMA((2,))],
        compiler_params=_cparams(("arbitrary",)),
        name="moe_dispatch",
    )(slot3, slot3, slot3, slot3, hpack)


def _expert_kernel(tile_ref, wa_ref, wb_ref, usea_ref, lo_ref, hi_ref, first_ref, newexp_ref,
                   valid_ref, x_ref, wga_ref, wua_ref, wda_ref, wgb_ref, wub_ref, wdb_ref,
                   o_ref, wg_bf, wu_bf, wd_bf):
    w = pl.program_id(0)
    bm, words = x_ref.shape

    @pl.when(valid_ref[w] == 1)
    def _():
        def load_weights(refs):
            wg_bf[...] = refs[0][...].astype(BF16)
            wu_bf[...] = refs[1][...].astype(BF16)
            wd_bf[...] = refs[2][...].astype(BF16)

        @pl.when(jnp.logical_and(newexp_ref[w] == 1, usea_ref[w] == 1))
        def _():
            load_weights((wga_ref, wua_ref, wda_ref))

        @pl.when(jnp.logical_and(newexp_ref[w] == 1, usea_ref[w] == 0))
        def _():
            load_weights((wgb_ref, wub_ref, wdb_ref))

        x_lo, x_hi = _unpack_bf16_pair(x_ref[...])
        x_lo, x_hi = x_lo.astype(BF16), x_hi.astype(BF16)
        gate = (jnp.dot(x_lo, wg_bf[0:words, :], preferred_element_type=F32)
                + jnp.dot(x_hi, wg_bf[words:, :], preferred_element_type=F32))
        up = (jnp.dot(x_lo, wu_bf[0:words, :], preferred_element_type=F32)
              + jnp.dot(x_hi, wu_bf[words:, :], preferred_element_type=F32))
        hid = (jax.nn.silu(gate) * up).astype(BF16)
        y = jnp.dot(hid, wd_bf[...], preferred_element_type=F32)
        packed = _pack_bf16_pair(y[:, :words], y[:, words:])
        row = tile_ref[w] * bm + lax.broadcasted_iota(jnp.int32, (bm, words), 0)
        mine = jnp.logical_and(row >= lo_ref[w], row < hi_ref[w])

        @pl.when(first_ref[w] == 1)
        def _():
            o_ref[...] = jnp.where(mine, packed, jnp.zeros_like(packed))

        @pl.when(first_ref[w] == 0)
        def _():
            o_ref[...] = jnp.where(mine, packed, o_ref[...])


def _experts(xs, work, n_work, w_gate, w_up, w_down):
    n_rows, words = xs.shape
    d_model, d_exp = w_gate.shape[1], w_gate.shape[2]
    bm = EXPERT_BLOCK
    names = ("tile", "wa", "wb", "usea", "lo", "hi", "first", "newexp", "valid")
    n_pre = len(names)
    row_map = lambda w, *pre: (pre[0][w], 0)
    stream = lambda i: (lambda w, *pre: (pre[i][w], 0, 0))
    up_spec = lambda i: pl.BlockSpec((None, d_model, d_exp), stream(i))
    down_spec = lambda i: pl.BlockSpec((None, d_exp, d_model), stream(i))
    grid_spec = pltpu.PrefetchScalarGridSpec(
        num_scalar_prefetch=n_pre,
        grid=(n_work,),
        in_specs=[pl.BlockSpec((bm, words), row_map),
                  up_spec(1), up_spec(1), down_spec(1),
                  up_spec(2), up_spec(2), down_spec(2)],
        out_specs=pl.BlockSpec((bm, words), row_map),
        scratch_shapes=[pltpu.VMEM((d_model, d_exp), BF16),
                        pltpu.VMEM((d_model, d_exp), BF16),
                        pltpu.VMEM((d_exp, d_model), BF16)])
    return pl.pallas_call(
        _expert_kernel,
        out_shape=jax.ShapeDtypeStruct((n_rows, words), U32),
        grid_spec=grid_spec,
        compiler_params=_cparams(("arbitrary",)),
        name="expert_ffn",
    )(*[work[n] for n in names], xs, w_gate, w_up, w_down, w_gate, w_up, w_down)


def _final_kernel(s0_ref, s1_ref, n0_ref, n1_ref, h_ref, gate_ref, g_ref, b_ref, ys_hbm,
                  o_ref, buf, sem):
    i = pl.program_id(0)
    n_steps = pl.num_programs(0)
    tm = h_ref.shape[0]
    cur = lax.rem(i, 2)

    def row_copy(r, k, s_ref, which):
        return pltpu.make_async_copy(ys_hbm.at[pl.ds(s_ref[0, r], 1), :],
                                     buf.at[which, k, pl.ds(r, 1), :], sem.at[which])

    def start_tile(refs, which):
        for r in range(tm):
            for k in range(TOP_K):
                row_copy(r, k, refs[k], which).start()

    def wait_tile(refs, which):
        for r in range(tm):
            for k in range(TOP_K):
                row_copy(r, k, refs[k], which).wait()

    @pl.when(i == 0)
    def _():
        start_tile((s0_ref, s1_ref), 0)

    wait_tile((s0_ref, s1_ref), cur)
    rows_per = tm // COMBINE_CHUNKS
    for c in range(COMBINE_CHUNKS):
        rs = slice(c * rows_per, (c + 1) * rows_per)
        gate = gate_ref[rs, :]
        y0_lo, y0_hi = _unpack_bf16_pair(buf[cur, 0, rs, :])
        y1_lo, y1_hi = _unpack_bf16_pair(buf[cur, 1, rs, :])
        g0, g1 = gate[:, 0:1], gate[:, 1:2]
        y = jnp.concatenate([y0_lo * g0 + y1_lo * g1, y0_hi * g0 + y1_hi * g1], axis=-1)
        o_ref[rs, :] = _layer_norm(ALPHA * h_ref[rs, :] + y, g_ref[...], b_ref[...])
        for r in range(c * rows_per, (c + 1) * rows_per):
            for k in range(TOP_K):
                row_copy(r, k, (n0_ref, n1_ref)[k], 1 - cur).start()

    @pl.when(i == n_steps - 1)
    def _():
        wait_tile((n0_ref, n1_ref), 1 - cur)


def _final(h1, ys, slot3, gate_t, ln_g, ln_b):
    n_tok, d_model = h1.shape
    words = ys.shape[1]
    tm = TOKEN_TILE
    n_tiles = n_tok // tm
    const = lambda shape: pl.BlockSpec(shape, lambda i: (0,) * len(shape))
    cur = lambda k: pl.BlockSpec((None, 1, tm), lambda i: (i + k * n_tiles, 0, 0),
                                 memory_space=pltpu.SMEM)
    nxt = lambda k: pl.BlockSpec(
        (None, 1, tm), lambda i: (jnp.minimum(i + 1, n_tiles - 1) + k * n_tiles, 0, 0),
        memory_space=pltpu.SMEM)
    return pl.pallas_call(
        _final_kernel,
        out_shape=jax.ShapeDtypeStruct((n_tok, d_model), F32),
        grid=(n_tiles,),
        in_specs=[cur(0), cur(1), nxt(0), nxt(1),
                  pl.BlockSpec((tm, d_model), lambda i: (i, 0)),
                  pl.BlockSpec((tm, TOP_K), lambda i: (i, 0)),
                  const(ln_g.shape), const(ln_b.shape),
                  pl.BlockSpec(memory_space=pl.ANY)],
        out_specs=pl.BlockSpec((tm, d_model), lambda i: (i, 0)),
        scratch_shapes=[pltpu.VMEM((2, TOP_K, tm, words), U32),
                        pltpu.SemaphoreType.DMA((2,))],
        compiler_params=_cparams(("arbitrary",)),
        name="combine_ln",
    )(slot3, slot3, slot3, slot3, h1, gate_t, ln_g, ln_b, ys)


def kernel(x, w_in, rpb, s5_a_re, s5_a_im, s5_log_dt, s5_b_re, s5_b_im, s5_c_re, s5_c_im, s5_d, w_glu, b_glu, w_out, ln1_g, ln1_b, w_router_group, b_router_group, w_router_expert, b_router_expert, w_gate, w_up, w_down, ln2_g, ln2_b):
    bsz, seq, d_model = x.shape
    n_tok = bsz * seq
    assert bsz == SUBLANES, "the S5 chunk scan keeps one batch element per sublane"
    assert seq % (ATTN_ROWS_PER_STEP * GRID_W) == 0 and n_tok % TOKEN_TILE == 0
    assert (TOP_K * n_tok) % EXPERT_BLOCK == 0
    assert (seq // S5_CHUNK) % LANES == 0, "each batch element owns whole lane tiles of chunks"
    assert w_in.shape[0] == DEPTH
    layer = 0
    x2d = x.reshape(n_tok, d_model)
    width = w_in.shape[-1] // 4
    n_heads = width // HEAD_DIM
    n_grp = s5_d.shape[1]

    w_in_bf = w_in[layer].astype(BF16)
    q, k, v = _in_proj(x2d, w_in_bf[:, :3 * width])
    ut = _u_proj(x2d, jnp.transpose(w_in_bf[:, 3 * width:]), n_grp)

    shape3 = (bsz, seq, width)
    attn = _attention(q.reshape(shape3), k.reshape(shape3), v.reshape(shape3),
                      _attn_bias_table(rpb[layer]), n_heads).reshape(n_tok, width)

    n_chunks = seq // S5_CHUNK
    lc = S5_CHUNK * S5_GROUP_CH
    w_mat, t_t, v_t, lam_re, lam_im = _s5_tables(
        s5_a_re[layer], s5_a_im[layer], s5_log_dt[layer], s5_b_re[layer], s5_b_im[layer],
        s5_c_re[layer], s5_c_im[layer], s5_d[layer])
    yt = _s5(ut.reshape(n_grp, lc, bsz * n_chunks), w_mat, t_t, v_t, lam_re, lam_im,
             bsz, n_chunks)
    glu = _glu(yt, jnp.transpose(w_glu[layer]).astype(BF16),
               b_glu[layer].astype(F32).reshape(-1, 1))

    n_router = N_EXPERT_GROUPS + N_EXPERTS
    w_router = jnp.concatenate([w_router_group[layer], w_router_expert[layer]], axis=1)
    w_router = jnp.pad(jnp.transpose(w_router), ((0, LANES - n_router), (0, 0))).astype(BF16)
    b_router = jnp.concatenate([b_router_group[layer], b_router_expert[layer]])
    b_router = jnp.pad(b_router, (0, LANES - n_router)).astype(F32).reshape(LANES, 1)
    row = lambda t: t.astype(F32).reshape(1, -1)
    h1, hpack, eid, gate, rank, hist = _mix(
        x2d, attn, glu, w_out[layer].astype(BF16), row(ln1_g[layer]), row(ln1_b[layer]),
        w_router, b_router)

    slot, work, n_work = _dispatch_plan(eid, rank, hist[:, :, 0], n_tok)
    slot3 = slot.reshape(TOP_K * (n_tok // TOKEN_TILE), 1, TOKEN_TILE)
    xs = _dispatch(hpack, slot3)
    ys = _experts(xs, work, n_work, w_gate[layer], w_up[layer], w_down[layer])

    out = _final(h1, ys, slot3, jnp.transpose(gate), row(ln2_g[layer]), row(ln2_b[layer]))
    return out.reshape(bsz, seq, d_model)
```

```python
import functools
import math

import numpy as np
import jax
import jax.numpy as jnp
from jax import lax
from jax.experimental import pallas as pl
from jax.experimental.pallas import tpu as pltpu

F32 = jnp.float32
BF16 = jnp.bfloat16
U32 = jnp.uint32

GRID_W = 64
WIN_H = 8
WIN_W = 16
HEAD_DIM = 64
N_EXPERT_GROUPS = 8
EXPERTS_PER_GROUP = 8
N_EXPERTS = N_EXPERT_GROUPS * EXPERTS_PER_GROUP
TOP_K = 2
S5_GROUP_CH = 16
S5_STATE = 64
LN_EPS = 1e-5
DEPTH = 1
ALPHA = (2.0 * DEPTH) ** 0.25

LANES = 128
SUBLANES = 8
VMEM_LIMIT_BYTES = 56 * 1024 * 1024

TOKEN_TILE = 512
S5_CHUNK = 16
S5_TOKEN_TILE = S5_CHUNK * LANES
S5_SWAP_ROWS = 64
ATTN_ROWS_PER_STEP = 8
EXPERT_BLOCK = 512
COMBINE_CHUNKS = 16
NEG_BIAS = -1e30


def _cparams(sem):
    return pltpu.CompilerParams(dimension_semantics=sem,
                                vmem_limit_bytes=VMEM_LIMIT_BYTES)


def _inproj_kernel(x_ref, w_ref, q_ref, k_ref, v_ref, *, width, scale):
    x = x_ref[...].astype(BF16)
    outs = (q_ref, k_ref, v_ref)
    for i, o_ref in enumerate(outs):
        acc = jnp.dot(x, w_ref[:, i * width:(i + 1) * width],
                      preferred_element_type=F32)
        if i == 0:
            acc = acc * scale
        o_ref[...] = acc.astype(BF16)


def _in_proj(x2d, w_qkv_bf):
    n_tok, d_model = x2d.shape
    width = w_qkv_bf.shape[1] // 3
    tm = TOKEN_TILE
    out = jax.ShapeDtypeStruct((n_tok, width), BF16)
    o_spec = pl.BlockSpec((tm, width), lambda i: (i, 0))
    return pl.pallas_call(
        functools.partial(_inproj_kernel, width=width, scale=HEAD_DIM ** -0.5),
        out_shape=(out, out, out),
        grid=(n_tok // tm,),
        in_specs=[pl.BlockSpec((tm, d_model), lambda i: (i, 0)),
                  pl.BlockSpec((d_model, 3 * width), lambda i: (0, 0))],
        out_specs=(o_spec, o_spec, o_spec),
        compiler_params=_cparams(("parallel",)),
        name="in_proj",
    )(x2d, w_qkv_bf)


def _uproj_kernel(x_ref, w_ref, o_ref):
    tm, d_model = x_ref.shape
    nc = tm // S5_CHUNK
    xp = pltpu.einshape("jtd->tjd", x_ref[...].reshape(nc, S5_CHUNK, d_model))
    xp = xp.reshape(tm, d_model).astype(BF16)
    ut = lax.dot_general(w_ref[...], xp, (((1,), (1,)), ((), ())),
                         preferred_element_type=F32)
    n_grp = o_ref.shape[0]
    for s in range(S5_CHUNK):
        o_ref[:, s, :, :] = ut[:, s * nc:(s + 1) * nc].astype(BF16).reshape(n_grp, S5_GROUP_CH, nc)


def _u_proj(x2d, w_u_t_bf, n_grp):
    n_tok, d_model = x2d.shape
    tm = S5_TOKEN_TILE
    nc = tm // S5_CHUNK
    return pl.pallas_call(
        _uproj_kernel,
        out_shape=jax.ShapeDtypeStruct((n_grp, S5_CHUNK, S5_GROUP_CH, n_tok // S5_CHUNK), BF16),
        grid=(n_tok // tm,),
        in_specs=[pl.BlockSpec((tm, d_model), lambda i: (i, 0)),
                  pl.BlockSpec(w_u_t_bf.shape, lambda i: (0, 0))],
        out_specs=pl.BlockSpec((n_grp, S5_CHUNK, S5_GROUP_CH, nc), lambda i: (0, 0, 0, i)),
        compiler_params=_cparams(("parallel",)),
        name="u_proj_t",
    )(x2d, w_u_t_bf)


def _attn_bias_table(rpb):
    col = np.arange(GRID_W)
    cstart = np.clip(col - WIN_W // 2, 0, GRID_W - WIN_W)
    j = np.arange(GRID_W)
    valid = (j[None, :] >= cstart[:, None]) & (j[None, :] < cstart[:, None] + WIN_W)
    dcol = j[None, :] - col[:, None] + (WIN_W - 1)
    pick = (dcol[..., None] == np.arange(2 * WIN_W - 1)) & valid[..., None]
    full = jnp.einsum('hdm,cjm->hdcj', rpb.astype(F32), jnp.asarray(pick, F32),
                      precision=lax.Precision.HIGHEST)
    full = jnp.where(jnp.asarray(valid)[None, None], full, NEG_BIAS)
    pair = jnp.concatenate([full[:, :-1], full[:, 1:]], axis=-1)
    n_heads = pair.shape[0]
    pair = pair.reshape(n_heads // 2, 2, 2 * WIN_H - 2, GRID_W, LANES)
    return jnp.transpose(pair, (2, 0, 1, 3, 4)).reshape(2 * WIN_H - 2, n_heads // 2,
                                                       2 * GRID_W, LANES)


def _attn_kernel(q_ref, k_ref, v_ref, b_ref, o_ref, *, n_rows, n_heads):
    rblk = pl.program_id(1)
    band = WIN_H * GRID_W
    n_pairs = n_heads // 2
    lane = lax.broadcasted_iota(jnp.int32, (GRID_W, LANES), 1)
    low_half = lane < HEAD_DIM

    def row_body(rr, carry):
        r = rblk * ATTN_ROWS_PER_STEP + rr
        rs = jnp.clip(r - WIN_H // 2, 0, n_rows - WIN_H)
        shift = rs - r + (WIN_H - 1)
        q_off = pl.multiple_of(rr * GRID_W, GRID_W)
        k_off = pl.multiple_of(rs * GRID_W, GRID_W)
        scores = []
        for hp in range(n_pairs):
            cols = slice(hp * LANES, (hp + 1) * LANES)
            qp = q_ref[pl.ds(q_off, GRID_W), cols]
            zero = jnp.zeros_like(qp)
            q2 = jnp.concatenate([jnp.where(low_half, qp, zero), jnp.where(low_half, zero, qp)],
                                 axis=0)
            kp = k_ref[pl.ds(k_off, band), cols]
            scores.append(lax.dot_general(q2, kp, (((1,), (1,)), ((), ())),
                                          preferred_element_type=F32))
        probs, dens = [], []
        for hp in range(n_pairs):
            s = jnp.concatenate(
                [scores[hp][:, m * LANES:(m + 1) * LANES] + b_ref[shift + 2 * m, hp]
                 for m in range(band // LANES)], axis=-1)
            mx = jnp.max(s, axis=-1, keepdims=True)
            p = jnp.exp(s - mx)
            dens.append(jnp.sum(p, axis=-1, keepdims=True))
            probs.append(p.astype(BF16))
        for hp in range(n_pairs):
            cols = slice(hp * LANES, (hp + 1) * LANES)
            vp = v_ref[pl.ds(k_off, band), cols]
            o = jnp.dot(probs[hp], vp, preferred_element_type=F32) / dens[hp]
            o_ref[pl.ds(q_off, GRID_W), cols] = jnp.where(
                low_half, o[:GRID_W], o[GRID_W:]).astype(BF16)
        return carry

    lax.fori_loop(0, ATTN_ROWS_PER_STEP, row_body, 0, unroll=True)


def _attention(q, k, v, bias_tab, n_heads):
    bsz, seq, width = q.shape
    n_rows = seq // GRID_W
    tq = ATTN_ROWS_PER_STEP * GRID_W
    return pl.pallas_call(
        functools.partial(_attn_kernel, n_rows=n_rows, n_heads=n_heads),
        out_shape=jax.ShapeDtypeStruct((bsz, seq, width), BF16),
        grid=(bsz, seq // tq),
        in_specs=[pl.BlockSpec((None, tq, width), lambda b, r: (b, r, 0)),
                  pl.BlockSpec((None, seq, width), lambda b, r: (b, 0, 0)),
                  pl.BlockSpec((None, seq, width), lambda b, r: (b, 0, 0)),
                  pl.BlockSpec(bias_tab.shape, lambda b, r: (0, 0, 0, 0))],
        out_specs=pl.BlockSpec((None, tq, width), lambda b, r: (b, r, 0)),
        compiler_params=_cparams(("parallel", "arbitrary")),
        name="nbr_attention",
    )(q, k, v, bias_tab)


def _s5_tables(a_re, a_im, log_dt, b_re, b_im, c_re, c_im, d_skip):
    hi = lax.Precision.HIGHEST
    L = S5_CHUNK
    n_grp, n_state = a_re.shape[1], a_re.shape[2]
    n_ch = b_re.shape[-1]
    f = lambda t: t.astype(F32)
    ks = jnp.arange(L + 1, dtype=F32)

    t_dirs, w_re, w_im, v_re, v_im, lam_l_re, lam_l_im = [], [], [], [], [], [], []
    for direction in range(2):
        ar, ai = f(a_re[direction]), f(a_im[direction])
        dt = jnp.exp(f(log_dt[direction]))[:, None]
        mag = jnp.exp(ar * dt)
        lr, li = mag * jnp.cos(ai * dt), mag * jnp.sin(ai * dt)
        den = ar * ar + ai * ai
        zr = ((lr - 1.0) * ar + li * ai) / den
        zi = (li * ar - (lr - 1.0) * ai) / den
        br, bi = f(b_re[direction]), f(b_im[direction])
        bbr = zr[..., None] * br - zi[..., None] * bi
        bbi = zr[..., None] * bi + zi[..., None] * br
        cr, ci = f(c_re[direction]), f(c_im[direction])
        pmag = jnp.exp(ks[:, None, None] * (ar * dt)[None])
        pr = pmag * jnp.cos(ks[:, None, None] * (ai * dt)[None])
        pi = pmag * jnp.sin(ks[:, None, None] * (ai * dt)[None])
        lbr = pr[..., None] * bbr[None] - pi[..., None] * bbi[None]
        lbi = pr[..., None] * bbi[None] + pi[..., None] * bbr[None]
        c_cat = jnp.concatenate([cr, -ci], axis=-1)
        lb_cat = jnp.concatenate([lbr, lbi], axis=2)
        lb_cat = jnp.transpose(lb_cat, (1, 2, 0, 3)).reshape(n_grp, 2 * n_state, (L + 1) * n_ch)
        kern = jnp.einsum('gcq,gqn->gcn', c_cat, lb_cat, precision=hi)
        kern = jnp.transpose(kern.reshape(n_grp, n_ch, L + 1, n_ch), (2, 0, 1, 3))
        s_idx = np.arange(L)[:, None]
        t_idx = np.arange(L)[None, :]
        lag = (t_idx - s_idx) if direction == 0 else (s_idx - t_idx)
        lag_pick = jnp.asarray(lag[..., None] == np.arange(L + 1), F32)
        tk = jnp.einsum('stk,kgcd->stgcd', lag_pick, kern, precision=hi)
        t_dirs.append(jnp.transpose(tk, (2, 0, 4, 1, 3)))
        lb_w = (lambda t: jnp.flip(t[:L], 0)) if direction == 0 else (lambda t: t[:L])
        w_re.append(jnp.transpose(lb_w(lbr), (1, 0, 3, 2)))
        w_im.append(jnp.transpose(lb_w(lbi), (1, 0, 3, 2)))
        p_o = (lambda t: t[1:]) if direction == 0 else (lambda t: jnp.flip(t[1:], 0))
        pro, pio = p_o(pr), p_o(pi)
        clr = cr[None] * pro[:, :, None, :] - ci[None] * pio[:, :, None, :]
        cli = cr[None] * pio[:, :, None, :] + ci[None] * pro[:, :, None, :]
        v_re.append(jnp.transpose(clr, (1, 3, 0, 2)))
        v_im.append(jnp.transpose(-cli, (1, 3, 0, 2)))
        lam_l_re.append(pr[L])
        lam_l_im.append(pi[L])

    lc = L * n_ch
    eye = jnp.eye(lc, dtype=F32).reshape(L, n_ch, L, n_ch)
    toep = t_dirs[0] + t_dirs[1] + eye[None] * f(d_skip)[:, None, None, None, :]
    toep = toep.reshape(n_grp, lc, lc)
    w_all = jnp.concatenate([w_re[0], w_re[1], w_im[0], w_im[1]], axis=-1)
    w_mat = w_all.reshape(n_grp, lc, 4 * n_state)
    v_mat = jnp.concatenate([v_re[0], v_re[1], v_im[0], v_im[1]], axis=1)
    v_mat = v_mat.reshape(n_grp, 4 * n_state, lc)
    lam_re = jnp.concatenate([lam_l_re[0], lam_l_re[1]], axis=-1)
    lam_im = jnp.concatenate([lam_l_im[0], lam_l_im[1]], axis=-1)
    bc = lambda t: jnp.broadcast_to(t[:, None, :], (n_grp, SUBLANES, 2 * n_state))
    return (w_mat.astype(BF16), jnp.swapaxes(toep, 1, 2).astype(BF16),
            jnp.swapaxes(v_mat, 1, 2).astype(BF16), bc(lam_re), bc(lam_im))


def _gelu_tanh(x):
    cdf = 0.5 * (1.0 + jnp.tanh(math.sqrt(2.0 / math.pi) * (x + 0.044715 * (x * x * x))))
    return x * cdf


def _s5_kernel(ut_ref, w_ref, tt_ref, vt_ref, lr_ref, li_ref, o_ref, xn_scr, x_scr, h_scr, hn_scr,
               *, n_chunks, n_state):
    n_batch = xn_scr.shape[0]
    sw = S5_SWAP_ROWS
    two_p = 2 * n_state

    for b in range(n_batch):
        cols = slice(b * n_chunks, (b + 1) * n_chunks)
        xn_scr[b] = lax.dot_general(ut_ref[:, cols], w_ref[...], (((0,), (0,)), ((), ())),
                                    preferred_element_type=F32)
    for jb in range(n_chunks // sw):
        rows = slice(jb * sw, (jb + 1) * sw)
        x_scr[rows] = pltpu.einshape("bjl->jbl", xn_scr[:, rows, :])

    lam_re = lr_ref[...]
    lam_im = li_ref[...]
    is_fwd = lax.broadcasted_iota(jnp.int32, (n_batch, two_p), 1) < n_state

    def scan_body(j, carry):
        h_re, h_im = carry
        jr = n_chunks - 1 - j
        x_re = jnp.where(is_fwd, x_scr[j, :, 0:two_p], x_scr[jr, :, 0:two_p])
        x_im = jnp.where(is_fwd, x_scr[j, :, two_p:2 * two_p], x_scr[jr, :, two_p:2 * two_p])
        h_scr[j, :, 0:n_state] = h_re[:, 0:n_state]
        h_scr[jr, :, n_state:two_p] = h_re[:, n_state:two_p]
        h_scr[j, :, two_p:two_p + n_state] = h_im[:, 0:n_state]
        h_scr[jr, :, two_p + n_state:2 * two_p] = h_im[:, n_state:two_p]
        n_re = lam_re * h_re - lam_im * h_im + x_re
        n_im = lam_re * h_im + lam_im * h_re + x_im
        return n_re, n_im

    zero = jnp.zeros((n_batch, two_p), F32)
    lax.fori_loop(0, n_chunks, scan_body, (zero, zero))

    for jb in range(n_chunks // sw):
        rows = slice(jb * sw, (jb + 1) * sw)
        hn_scr[:, rows, :] = pltpu.einshape("jbl->bjl", h_scr[rows])
    for b in range(n_batch):
        cols = slice(b * n_chunks, (b + 1) * n_chunks)
        y = (jnp.dot(tt_ref[...], ut_ref[:, cols], preferred_element_type=F32)
             + lax.dot_general(vt_ref[...], hn_scr[b].astype(BF16), (((1,), (1,)), ((), ())),
                               preferred_element_type=F32))
        o_ref[:, cols] = _gelu_tanh(y).astype(BF16)


def _s5(ut, w_mat, t_t, v_t, lam_re, lam_im, n_batch, n_chunks):
    n_grp, lc, n_cols = ut.shape
    four_p = w_mat.shape[2]
    n_state = four_p // 4
    grp = lambda shape: pl.BlockSpec((None,) + shape, lambda g: (g, 0, 0))
    return pl.pallas_call(
        functools.partial(_s5_kernel, n_chunks=n_chunks, n_state=n_state),
        out_shape=jax.ShapeDtypeStruct((n_grp, lc, n_cols), BF16),
        grid=(n_grp,),
        in_specs=[grp((lc, n_cols)), grp((lc, four_p)), grp((lc, lc)), grp((lc, four_p)),
                  grp((SUBLANES, 2 * n_state)), grp((SUBLANES, 2 * n_state))],
        out_specs=grp((lc, n_cols)),
        scratch_shapes=[pltpu.VMEM((n_batch, n_chunks, four_p), F32),
                        pltpu.VMEM((n_chunks, n_batch, four_p), F32),
                        pltpu.VMEM((n_chunks, n_batch, four_p), F32),
                        pltpu.VMEM((n_batch, n_chunks, four_p), F32)],
        compiler_params=_cparams(("parallel",)),
        name="s5_chunked",
    )(ut, w_mat, t_t, v_t, lam_re, lam_im)


def _glu_kernel(yt_ref, wt_ref, b_ref, o_ref):
    n_grp, lc, nc = yt_ref.shape
    n_ch = lc // S5_CHUNK
    width = n_grp * n_ch
    ssm_t = jnp.concatenate(
        [yt_ref[:, t * n_ch:(t + 1) * n_ch, :].reshape(width, nc) for t in range(S5_CHUNK)], axis=1)
    z = jnp.dot(wt_ref[...], ssm_t, preferred_element_type=F32) + b_ref[...]
    glu_t = ssm_t.astype(F32) * jax.nn.sigmoid(z)
    glu = jnp.transpose(glu_t).reshape(S5_CHUNK, nc, width)
    o_ref[...] = pltpu.einshape("tjd->jtd", glu).reshape(S5_CHUNK * nc, width).astype(BF16)


def _glu(yt, w_glu_t_bf, b_glu_col):
    n_grp, lc, n_cols = yt.shape
    nc = LANES
    width = w_glu_t_bf.shape[0]
    return pl.pallas_call(
        _glu_kernel,
        out_shape=jax.ShapeDtypeStruct((n_cols * S5_CHUNK, width), BF16),
        grid=(n_cols // nc,),
        in_specs=[pl.BlockSpec((n_grp, lc, nc), lambda i: (0, 0, i)),
                  pl.BlockSpec(w_glu_t_bf.shape, lambda i: (0, 0)),
                  pl.BlockSpec(b_glu_col.shape, lambda i: (0, 0))],
        out_specs=pl.BlockSpec((nc * S5_CHUNK, width), lambda i: (i, 0)),
        compiler_params=_cparams(("parallel",)),
        name="glu_to_tokens",
    )(yt, w_glu_t_bf, b_glu_col)


def _layer_norm(x, g, b):
    mu = jnp.mean(x, axis=-1, keepdims=True)
    xc = x - mu
    var = jnp.mean(xc * xc, axis=-1, keepdims=True)
    return xc * lax.rsqrt(var + LN_EPS) * g + b


def _pack_bf16_pair(lo, hi):
    lo_bits = lax.bitcast_convert_type(lo.astype(BF16).astype(F32), U32) >> 16
    hi_bits = lax.bitcast_convert_type(hi.astype(BF16).astype(F32), U32) & jnp.uint32(0xFFFF0000)
    return lo_bits | hi_bits


def _unpack_bf16_pair(word):
    lo = lax.bitcast_convert_type(word << 16, F32)
    hi = lax.bitcast_convert_type(word & jnp.uint32(0xFFFF0000), F32)
    return lo, hi


def _first_index_of_max(vals, row_id, n):
    mx = jnp.max(vals, axis=0, keepdims=True)
    idx = jnp.min(jnp.where(vals == mx, row_id, n), axis=0, keepdims=True)
    return mx, idx


def _mix_kernel(x_ref, a_ref, s_ref, wout_ref, g_ref, b_ref,
                wr_ref, br_ref, tri_ref, h_ref, hp_ref, eid_ref, gate_ref, rank_ref, hist_ref):
    half = a_ref.shape[1]
    mix = (jnp.dot(a_ref[...], wout_ref[0:half, :], preferred_element_type=F32)
           + jnp.dot(s_ref[...], wout_ref[half:, :], preferred_element_type=F32))
    h = _layer_norm(ALPHA * x_ref[...] + mix, g_ref[...], b_ref[...])
    h_ref[...] = h
    d_half = h.shape[1] // 2
    hp_ref[...] = _pack_bf16_pair(h[:, :d_half], h[:, d_half:])

    logits = lax.dot_general(wr_ref[...], h.astype(BF16), (((1,), (1,)), ((), ())),
                             preferred_element_type=F32) + br_ref[...]
    ng, epg = N_EXPERT_GROUPS, EXPERTS_PER_GROUP
    tm = logits.shape[1]
    row_id = lax.broadcasted_iota(jnp.int32, (ng, tm), 0)
    g_logit = logits[0:ng, :]
    g_max, g_idx = _first_index_of_max(g_logit, row_id, ng)
    g_val = 1.0 / jnp.sum(jnp.exp(g_logit - g_max), axis=0, keepdims=True)
    e_in = jnp.zeros((epg, tm), F32)
    for gi in range(ng):
        e_in = jnp.where(g_idx == gi, logits[ng + gi * epg:ng + (gi + 1) * epg, :], e_in)
    m1, i1 = _first_index_of_max(e_in, row_id, epg)
    rest = jnp.where(row_id == i1, -jnp.inf, e_in)
    m2, i2 = _first_index_of_max(rest, row_id, epg)
    e2 = jnp.exp(m2 - m1)
    w1 = 1.0 / (1.0 + e2)
    w2 = e2 / (1.0 + e2)
    eids = (g_idx * epg + i1, g_idx * epg + i2)
    gate_ref[0:1, :] = g_val * w1
    gate_ref[1:2, :] = g_val * w2

    exp_id = lax.broadcasted_iota(jnp.int32, (N_EXPERTS, tm), 0)
    before = jnp.zeros((N_EXPERTS, 1), F32)
    for kk in range(TOP_K):
        onehot = exp_id == eids[kk]
        prefix = jnp.dot(onehot.astype(BF16), tri_ref[...], preferred_element_type=F32)
        rank = jnp.sum(jnp.where(onehot, prefix + before, 0.0), axis=0, keepdims=True)
        eid_ref[kk:kk + 1, :] = eids[kk]
        rank_ref[kk:kk + 1, :] = rank.astype(jnp.int32)
        before = before + jnp.sum(onehot.astype(F32), axis=1, keepdims=True)
    hist_ref[...] = jnp.broadcast_to(before, hist_ref.shape).astype(jnp.int32)


def _mix(x2d, attn, glu, w_out, ln_g, ln_b, w_router, b_router):
    n_tok, d_model = x2d.shape
    half = attn.shape[1]
    tm = TOKEN_TILE
    n_tiles = n_tok // tm
    tri = jnp.asarray(np.triu(np.ones((tm, tm), np.float32), k=1), dtype=BF16)
    const = lambda shape: pl.BlockSpec(shape, lambda i: (0,) * len(shape))
    sel = lambda: pl.BlockSpec((TOP_K, tm), lambda i: (0, i))
    return pl.pallas_call(
        _mix_kernel,
        out_shape=(jax.ShapeDtypeStruct((n_tok, d_model), F32),
                   jax.ShapeDtypeStruct((n_tok, d_model // 2), U32),
                   jax.ShapeDtypeStruct((TOP_K, n_tok), jnp.int32),
                   jax.ShapeDtypeStruct((TOP_K, n_tok), F32),
                   jax.ShapeDtypeStruct((TOP_K, n_tok), jnp.int32),
                   jax.ShapeDtypeStruct((n_tiles, N_EXPERTS, LANES), jnp.int32)),
        grid=(n_tiles,),
        in_specs=[pl.BlockSpec((tm, d_model), lambda i: (i, 0)),
                  pl.BlockSpec((tm, half), lambda i: (i, 0)),
                  pl.BlockSpec((tm, half), lambda i: (i, 0)),
                  const(w_out.shape), const(ln_g.shape), const(ln_b.shape),
                  const(w_router.shape), const(b_router.shape), const(tri.shape)],
        out_specs=(pl.BlockSpec((tm, d_model), lambda i: (i, 0)),
                   pl.BlockSpec((tm, d_model // 2), lambda i: (i, 0)),
                   sel(), sel(), sel(),
                   pl.BlockSpec((None, N_EXPERTS, LANES), lambda i: (i, 0, 0))),
        compiler_params=_cparams(("parallel",)),
        name="mix_ln_router",
    )(x2d, attn, glu, w_out, ln_g, ln_b, w_router, b_router, tri)


def _dispatch_plan(eid, rank, hist, n_tok):
    bm = EXPERT_BLOCK
    n_assign = TOP_K * n_tok
    tm = TOKEN_TILE
    n_tiles = n_tok // tm
    counts = jnp.sum(hist, axis=0)
    ends = jnp.cumsum(counts)
    off = ends - counts
    base = off[None, :] + jnp.cumsum(hist, axis=0) - hist
    eid_t = eid.reshape(TOP_K, n_tiles, tm)
    onehot = eid_t[..., None] == jnp.arange(N_EXPERTS, dtype=jnp.int32)
    slot = rank.reshape(TOP_K, n_tiles, tm) + jnp.sum(
        jnp.where(onehot, base[None, :, None, :], 0), axis=-1)
    slot = slot.astype(jnp.int32)

    n_mtiles = n_assign // bm
    n_work = n_mtiles + N_EXPERTS
    t_lo = off // bm
    n_vis = jnp.where(counts > 0, (ends + bm - 1) // bm - t_lo, 0)
    w_end = jnp.cumsum(n_vis)
    w_start = w_end - n_vis
    n_valid = w_end[-1]
    w = jnp.arange(n_work, dtype=jnp.int32)
    wc = jnp.minimum(w, n_valid - 1)
    e_w = jnp.minimum(jnp.sum((w_end[None, :] <= wc[:, None]).astype(jnp.int32), axis=1),
                      N_EXPERTS - 1)
    pick = e_w[:, None] == jnp.arange(N_EXPERTS, dtype=jnp.int32)[None, :]
    take = lambda t: jnp.sum(jnp.where(pick, t[None, :], 0), axis=1)
    tile_w = (take(t_lo) + (wc - take(w_start))).astype(jnp.int32)
    valid = w < n_valid
    lo_w = jnp.where(valid, take(off), 0).astype(jnp.int32)
    hi_w = jnp.where(valid, take(ends), 0).astype(jnp.int32)
    prev_tile = jnp.concatenate([jnp.full((1,), -1, jnp.int32), tile_w[:-1]])
    first_w = (tile_w != prev_tile).astype(jnp.int32)
    prev_e = jnp.concatenate([jnp.full((1,), -1, jnp.int32), e_w[:-1]])
    newexp_w = (e_w != prev_e).astype(jnp.int32)
    work = dict(tile=tile_w, expert=e_w, lo=lo_w, hi=hi_w, first=first_w, newexp=newexp_w,
                valid=valid.astype(jnp.int32))
    return slot, work, n_work


def _dispatch_kernel(s0_ref, s1_ref, p0_ref, p1_ref, hp_ref, xs_hbm, ring, sem):
    i = pl.program_id(0)
    n_steps = pl.num_programs(0)
    tm = hp_ref.shape[0]
    cur = lax.rem(i, 2)

    def row_copy(r, s_ref, which):
        return pltpu.make_async_copy(ring.at[which, pl.ds(r, 1), :],
                                     xs_hbm.at[pl.ds(s_ref[0, r], 1), :], sem.at[which])

    def wait_tile(refs, which):
        for r in range(tm):
            for s_ref in refs:
                row_copy(r, s_ref, which).wait()

    ring[cur] = hp_ref[...]
    for r in range(tm):
        row_copy(r, s0_ref, cur).start(priority=0)
        row_copy(r, s1_ref, cur).start(priority=1)

    @pl.when(i >= 1)
    def _():
        wait_tile((p0_ref, p1_ref), 1 - cur)

    @pl.when(i == n_steps - 1)
    def _():
        wait_tile((s0_ref, s1_ref), cur)


def _dispatch(hpack, slot3):
    n_tok, words = hpack.shape
    tm = TOKEN_TILE
    n_tiles = n_tok // tm
    smem = lambda k, d: pl.BlockSpec(
        (None, 1, tm), lambda i: (jnp.maximum(i - d, 0) + k * n_tiles, 0, 0),
        memory_space=pltpu.SMEM)
    return pl.pallas_call(
        _dispatch_kernel,
        out_shape=jax.ShapeDtypeStruct((TOP_K * n_tok, words), U32),
        grid=(n_tiles,),
        in_specs=[smem(0, 0), smem(1, 0), smem(0, 1), smem(1, 1),
                  pl.BlockSpec((tm, words), lambda i: (i, 0))],
        out_specs=pl.BlockSpec(memory_space=pl.ANY),
        scratch_shapes=[pltpu.VMEM((2, tm, words), U32),
                        pltpu.SemaphoreType.DMA((2,))],
        compiler_params=_cparams(("arbitrary",)),
        name="moe_dispatch",
    )(slot3, slot3, slot3, slot3, hpack)


def _expert_kernel(tile_ref, exp_ref, lo_ref, hi_ref, first_ref, newexp_ref, valid_ref,
                   x_ref, wg_ref, wu_ref, wd_ref, o_ref, wg_bf, wu_bf, wd_bf):
    w = pl.program_id(0)
    bm, words = x_ref.shape

    @pl.when(valid_ref[w] == 1)
    def _():
        @pl.when(newexp_ref[w] == 1)
        def _():
            wg_bf[...] = wg_ref[...].astype(BF16)
            wu_bf[...] = wu_ref[...].astype(BF16)
            wd_bf[...] = wd_ref[...].astype(BF16)

        x_lo, x_hi = _unpack_bf16_pair(x_ref[...])
        x_lo, x_hi = x_lo.astype(BF16), x_hi.astype(BF16)
        gate = (jnp.dot(x_lo, wg_bf[0:words, :], preferred_element_type=F32)
                + jnp.dot(x_hi, wg_bf[words:, :], preferred_element_type=F32))
        up = (jnp.dot(x_lo, wu_bf[0:words, :], preferred_element_type=F32)
              + jnp.dot(x_hi, wu_bf[words:, :], preferred_element_type=F32))
        hid = (jax.nn.silu(gate) * up).astype(BF16)
        y = jnp.dot(hid, wd_bf[...], preferred_element_type=F32)
        packed = _pack_bf16_pair(y[:, :words], y[:, words:])
        row = tile_ref[w] * bm + lax.broadcasted_iota(jnp.int32, (bm, words), 0)
        mine = jnp.logical_and(row >= lo_ref[w], row < hi_ref[w])

        @pl.when(first_ref[w] == 1)
        def _():
            o_ref[...] = jnp.where(mine, packed, jnp.zeros_like(packed))

        @pl.when(first_ref[w] == 0)
        def _():
            o_ref[...] = jnp.where(mine, packed, o_ref[...])


def _experts(xs, work, n_work, w_gate, w_up, w_down):
    n_rows, words = xs.shape
    d_model, d_exp = w_gate.shape[1], w_gate.shape[2]
    bm = EXPERT_BLOCK
    names = ("tile", "expert", "lo", "hi", "first", "newexp", "valid")
    n_pre = len(names)
    row_map = lambda w, *pre: (pre[0][w], 0)
    exp_map = lambda w, *pre: (pre[1][w], 0, 0)
    grid_spec = pltpu.PrefetchScalarGridSpec(
        num_scalar_prefetch=n_pre,
        grid=(n_work,),
        in_specs=[pl.BlockSpec((bm, words), row_map),
                  pl.BlockSpec((None, d_model, d_exp), exp_map),
                  pl.BlockSpec((None, d_model, d_exp), exp_map),
                  pl.BlockSpec((None, d_exp, d_model), exp_map)],
        out_specs=pl.BlockSpec((bm, words), row_map),
        scratch_shapes=[pltpu.VMEM((d_model, d_exp), BF16),
                        pltpu.VMEM((d_model, d_exp), BF16),
                        pltpu.VMEM((d_exp, d_model), BF16)])
    return pl.pallas_call(
        _expert_kernel,
        out_shape=jax.ShapeDtypeStruct((n_rows, words), U32),
        grid_spec=grid_spec,
        compiler_params=_cparams(("arbitrary",)),
        name="expert_ffn",
    )(*[work[n] for n in names], xs, w_gate, w_up, w_down)


def _final_kernel(s0_ref, s1_ref, n0_ref, n1_ref, h_ref, gate_ref, g_ref, b_ref, ys_hbm,
                  o_ref, buf, sem):
    i = pl.program_id(0)
    n_steps = pl.num_programs(0)
    tm = h_ref.shape[0]
    cur = lax.rem(i, 2)

    def row_copy(r, k, s_ref, which):
        return pltpu.make_async_copy(ys_hbm.at[pl.ds(s_ref[0, r], 1), :],
                                     buf.at[which, k, pl.ds(r, 1), :], sem.at[which])

    def start_tile(refs, which):
        for r in range(tm):
            for k in range(TOP_K):
                row_copy(r, k, refs[k], which).start()

    def wait_tile(refs, which):
        for r in range(tm):
            for k in range(TOP_K):
                row_copy(r, k, refs[k], which).wait()

    @pl.when(i == 0)
    def _():
        start_tile((s0_ref, s1_ref), 0)

    wait_tile((s0_ref, s1_ref), cur)
    rows_per = tm // COMBINE_CHUNKS
    for c in range(COMBINE_CHUNKS):
        rs = slice(c * rows_per, (c + 1) * rows_per)
        gate = gate_ref[rs, :]
        y0_lo, y0_hi = _unpack_bf16_pair(buf[cur, 0, rs, :])
        y1_lo, y1_hi = _unpack_bf16_pair(buf[cur, 1, rs, :])
        g0, g1 = gate[:, 0:1], gate[:, 1:2]
        y = jnp.concatenate([y0_lo * g0 + y1_lo * g1, y0_hi * g0 + y1_hi * g1], axis=-1)
        o_ref[rs, :] = _layer_norm(ALPHA * h_ref[rs, :] + y, g_ref[...], b_ref[...])
        for r in range(c * rows_per, (c + 1) * rows_per):
            for k in range(TOP_K):
                row_copy(r, k, (n0_ref, n1_ref)[k], 1 - cur).start()

    @pl.when(i == n_steps - 1)
    def _():
        wait_tile((n0_ref, n1_ref), 1 - cur)


def _final(h1, ys, slot3, gate_t, ln_g, ln_b):
    n_tok, d_model = h1.shape
    words = ys.shape[1]
    tm = TOKEN_TILE
    n_tiles = n_tok // tm
    const = lambda shape: pl.BlockSpec(shape, lambda i: (0,) * len(shape))
    cur = lambda k: pl.BlockSpec((None, 1, tm), lambda i: (i + k * n_tiles, 0, 0),
                                 memory_space=pltpu.SMEM)
    nxt = lambda k: pl.BlockSpec(
        (None, 1, tm), lambda i: (jnp.minimum(i + 1, n_tiles - 1) + k * n_tiles, 0, 0),
        memory_space=pltpu.SMEM)
    return pl.pallas_call(
        _final_kernel,
        out_shape=jax.ShapeDtypeStruct((n_tok, d_model), F32),
        grid=(n_tiles,),
        in_specs=[cur(0), cur(1), nxt(0), nxt(1),
                  pl.BlockSpec((tm, d_model), lambda i: (i, 0)),
                  pl.BlockSpec((tm, TOP_K), lambda i: (i, 0)),
                  const(ln_g.shape), const(ln_b.shape),
                  pl.BlockSpec(memory_space=pl.ANY)],
        out_specs=pl.BlockSpec((tm, d_model), lambda i: (i, 0)),
        scratch_shapes=[pltpu.VMEM((2, TOP_K, tm, words), U32),
                        pltpu.SemaphoreType.DMA((2,))],
        compiler_params=_cparams(("arbitrary",)),
        name="combine_ln",
    )(slot3, slot3, slot3, slot3, h1, gate_t, ln_g, ln_b, ys)


def kernel(x, w_in, rpb, s5_a_re, s5_a_im, s5_log_dt, s5_b_re, s5_b_im, s5_c_re, s5_c_im, s5_d, w_glu, b_glu, w_out, ln1_g, ln1_b, w_router_group, b_router_group, w_router_expert, b_router_expert, w_gate, w_up, w_down, ln2_g, ln2_b):
    bsz, seq, d_model = x.shape
    n_tok = bsz * seq
    assert bsz == SUBLANES, "the S5 chunk scan keeps one batch element per sublane"
    assert seq % (ATTN_ROWS_PER_STEP * GRID_W) == 0 and n_tok % TOKEN_TILE == 0
    assert (TOP_K * n_tok) % EXPERT_BLOCK == 0
    assert (seq // S5_CHUNK) % LANES == 0, "each batch element owns whole lane tiles of chunks"
    assert w_in.shape[0] == DEPTH
    layer = 0
    x2d = x.reshape(n_tok, d_model)
    width = w_in.shape[-1] // 4
    n_heads = width // HEAD_DIM
    n_grp = s5_d.shape[1]

    w_in_bf = w_in[layer].astype(BF16)
    q, k, v = _in_proj(x2d, w_in_bf[:, :3 * width])
    ut = _u_proj(x2d, jnp.transpose(w_in_bf[:, 3 * width:]), n_grp)

    shape3 = (bsz, seq, width)
    attn = _attention(q.reshape(shape3), k.reshape(shape3), v.reshape(shape3),
                      _attn_bias_table(rpb[layer]), n_heads).reshape(n_tok, width)

    n_chunks = seq // S5_CHUNK
    lc = S5_CHUNK * S5_GROUP_CH
    w_mat, t_t, v_t, lam_re, lam_im = _s5_tables(
        s5_a_re[layer], s5_a_im[layer], s5_log_dt[layer], s5_b_re[layer], s5_b_im[layer],
        s5_c_re[layer], s5_c_im[layer], s5_d[layer])
    yt = _s5(ut.reshape(n_grp, lc, bsz * n_chunks), w_mat, t_t, v_t, lam_re, lam_im,
             bsz, n_chunks)
    glu = _glu(yt, jnp.transpose(w_glu[layer]).astype(BF16),
               b_glu[layer].astype(F32).reshape(-1, 1))

    n_router = N_EXPERT_GROUPS + N_EXPERTS
    w_router = jnp.concatenate([w_router_group[layer], w_router_expert[layer]], axis=1)
    w_router = jnp.pad(jnp.transpose(w_router), ((0, LANES - n_router), (0, 0))).astype(BF16)
    b_router = jnp.concatenate([b_router_group[layer], b_router_expert[layer]])
    b_router = jnp.pad(b_router, (0, LANES - n_router)).astype(F32).reshape(LANES, 1)
    row = lambda t: t.astype(F32).reshape(1, -1)
    h1, hpack, eid, gate, rank, hist = _mix(
        x2d, attn, glu, w_out[layer].astype(BF16), row(ln1_g[layer]), row(ln1_b[layer]),
        w_router, b_router)

    slot, work, n_work = _dispatch_plan(eid, rank, hist[:, :, 0], n_tok)
    slot3 = slot.reshape(TOP_K * (n_tok // TOKEN_TILE), 1, TOKEN_TILE)
    xs = _dispatch(hpack, slot3)
    ys = _experts(xs, work, n_work, w_gate[layer], w_up[layer], w_down[layer])

    out = _final(h1, ys, slot3, jnp.transpose(gate), row(ln2_g[layer]), row(ln2_b[layer]))
    return out.reshape(bsz, seq, d_model)
```
